```python
import math
import numpy as np
import jax
import jax.numpy as jnp
from jax import lax

D_MODEL = 1024
BATCH = 16
SEQ = 2048
DEPTH = 2

CTX_LEN = 256
GRID_W = 64
HEAD_DIM = 64
ROPE_HALF = HEAD_DIM // 2
ROPE_THETA = 10000.0
A_HEADS = 8
A_KV_HEADS = 2
B_HEADS = 4
C_HEADS = 16
C_KV_HEADS = 4
WINDOW = 128
Q_BLOCK = 128
D_FF = 2816
N_SUB = 3
HALF_STEP = 0.5
EPS = 1e-6
N_EVEN = (DEPTH + 1) // 2
N_ODD = DEPTH // 2
AB_WIDTHS = (A_HEADS * HEAD_DIM, A_KV_HEADS * HEAD_DIM, A_KV_HEADS * HEAD_DIM,
             B_HEADS * 2 * HEAD_DIM, B_HEADS * 2 * HEAD_DIM, B_HEADS * 2 * HEAD_DIM)
AB_IN = sum(AB_WIDTHS)
AB_OUT = A_HEADS * HEAD_DIM + B_HEADS * 2 * HEAD_DIM
C_WIDTHS = (C_HEADS * HEAD_DIM, C_KV_HEADS * HEAD_DIM, C_KV_HEADS * HEAD_DIM)
C_IN = sum(C_WIDTHS)
C_OUT = C_HEADS * HEAD_DIM

kernel_name = 'hybrid_dit_gqa_diffattn_swa_macaron'


def _rms(x, g):
    xf = x.astype(jnp.float32)
    y = xf * lax.rsqrt(jnp.mean(xf * xf, axis=-1, keepdims=True) + EPS)
    return y.astype(x.dtype) * g


def _pre(x, g, shift, scale):
    return _rms(x, g) * (1 + scale) + shift


def _residual(x, y, g_post, gate, res_w):
    return x + res_w * gate * _rms(y, g_post)


def _swiglu(h, wg, wu, wd):
    return (jax.nn.silu(h @ wg) * (h @ wu)) @ wd


def _split(p, widths):
    out, start = [], 0
    for w in widths:
        out.append(p[..., start:start + w])
        start += w
    return out


def _axial_rope(L):
    rows = L // GRID_W
    row = jnp.repeat(jnp.arange(rows, dtype=jnp.float32), GRID_W)
    col = jnp.tile(jnp.arange(GRID_W, dtype=jnp.float32), rows)
    inv = ROPE_THETA ** (-jnp.arange(0, ROPE_HALF, 2, dtype=jnp.float32) / ROPE_HALF)
    ang = jnp.concatenate([row[:, None] * inv, col[:, None] * inv], axis=-1)
    return jnp.cos(ang), jnp.sin(ang)


def _rope(x, cos, sin):
    shape = (1, x.shape[1]) + (1,) * (x.ndim - 3) + (ROPE_HALF,)
    cs, sn = cos.reshape(shape), sin.reshape(shape)
    x1, x2 = x[..., :ROPE_HALF], x[..., ROPE_HALF:]
    return jnp.concatenate([x1 * cs - x2 * sn, x2 * cs + x1 * sn], axis=-1).astype(x.dtype)


def _softmax_with_sink(s, sink):
    sk = sink[None, :, :, None, None]
    m = jnp.maximum(jnp.max(s, axis=-1, keepdims=True), sk)
    p = jnp.exp(s - m)
    return p / (jnp.sum(p, axis=-1, keepdims=True) + jnp.exp(sk - m))


def _gqa_dense(q, k, v, sink=None):
    B, L, H, dh = q.shape
    Hkv = k.shape[2]
    G = H // Hkv
    nb = L // Q_BLOCK
    scale = dh ** -0.5
    qb = q.reshape(B, nb, Q_BLOCK, Hkv, G, dh).swapaxes(0, 1)

    def one(qblk):
        s = jnp.einsum('bqkgd,bskd->bkgqs', qblk, k).astype(jnp.float32) * scale
        if sink is None:
            p = jax.nn.softmax(s, axis=-1)
        else:
            p = _softmax_with_sink(s, sink.reshape(Hkv, G).astype(jnp.float32))
        return jnp.einsum('bkgqs,bskd->bqkgd', p.astype(v.dtype), v)

    o = lax.map(one, qb)
    return o.swapaxes(0, 1).reshape(B, L, H * dh)


def _diff_attn(q, k, v, lam, sub_gain, lam_init):
    B, L, H, _, dh = q.shape
    nb = L // Q_BLOCK
    scale = dh ** -0.5
    qb = q.reshape(B, nb, Q_BLOCK, H, 2, dh).swapaxes(0, 1)

    def one(qblk):
        s = jnp.einsum('bqhcd,bshcd->bhcqs', qblk, k).astype(jnp.float32) * scale
        p = jax.nn.softmax(s, axis=-1)
        a = p[:, :, 0] - lam * p[:, :, 1]
        return jnp.einsum('bhqs,bshe->bqhe', a.astype(v.dtype), v)

    o = lax.map(one, qb).swapaxes(0, 1).reshape(B, L, H, 2 * dh)
    o = _rms(o, sub_gain) * (1.0 - lam_init)
    return o.reshape(B, L, H * 2 * dh)


def _window_attn(q, k, v, k_ctx, v_ctx, sink):
    B, L, H, dh = q.shape
    Hkv = k.shape[2]
    G = H // Hkv
    C = k_ctx.shape[1]
    nb = L // Q_BLOCK
    band = Q_BLOCK + 2 * WINDOW
    scale = dh ** -0.5
    pad = ((0, 0), (WINDOW, WINDOW), (0, 0), (0, 0))
    k_pad, v_pad = jnp.pad(k, pad), jnp.pad(v, pad)
    sk = sink.reshape(Hkv, G).astype(jnp.float32)

    def one(blk):
        start = blk * Q_BLOCK
        qblk = lax.dynamic_slice_in_dim(q, start, Q_BLOCK, axis=1).reshape(B, Q_BLOCK, Hkv, G, dh)
        kb = lax.dynamic_slice_in_dim(k_pad, start, band, axis=1)
        vb = lax.dynamic_slice_in_dim(v_pad, start, band, axis=1)
        qpos = start + jnp.arange(Q_BLOCK)
        kpos = start - WINDOW + jnp.arange(band)
        valid = (jnp.abs(kpos[None, :] - qpos[:, None]) <= WINDOW) & (kpos >= 0) & (kpos < L)
        s_loc = jnp.einsum('bqkgd,bskd->bkgqs', qblk, kb).astype(jnp.float32) * scale
        s_loc = jnp.where(valid, s_loc, -jnp.inf)
        s_ctx = jnp.einsum('bqkgd,bckd->bkgqc', qblk, k_ctx).astype(jnp.float32) * scale
        p = _softmax_with_sink(jnp.concatenate([s_ctx, s_loc], axis=-1), sk)
        o = (jnp.einsum('bkgqc,bckd->bqkgd', p[..., :C].astype(v.dtype), v_ctx)
             + jnp.einsum('bkgqs,bskd->bqkgd', p[..., C:].astype(v.dtype), vb))
        return o.reshape(B, Q_BLOCK, H * dh)

    o = lax.map(one, jnp.arange(nb))
    return o.swapaxes(0, 1).reshape(B, L, H * dh)


def _split_ab(p):
    B, S, _ = p.shape
    aq, ak, av, bq, bk, bv = _split(p, AB_WIDTHS)
    return (aq.reshape(B, S, A_HEADS, HEAD_DIM), ak.reshape(B, S, A_KV_HEADS, HEAD_DIM),
            av.reshape(B, S, A_KV_HEADS, HEAD_DIM), bq.reshape(B, S, B_HEADS, 2, HEAD_DIM),
            bk.reshape(B, S, B_HEADS, 2, HEAD_DIM), bv.reshape(B, S, B_HEADS, 2 * HEAD_DIM))


def _split_c(p):
    B, S, _ = p.shape
    q, k, v = _split(p, C_WIDTHS)
    return (q.reshape(B, S, C_HEADS, HEAD_DIM), k.reshape(B, S, C_KV_HEADS, HEAD_DIM),
            v.reshape(B, S, C_KV_HEADS, HEAD_DIM))


def _mixer_ab(h_lat, h_ctx, w_in, w_out, q_gain, k_gain, lq1, lk1, lq2, lk2, sub_gain, lam_init,
              cos, sin, ctx_out):
    aq_l, ak_l, av_l, bq_l, bk_l, bv_l = _split_ab(h_lat @ w_in)
    aq_c, ak_c, av_c, bq_c, bk_c, bv_c = _split_ab(h_ctx @ w_in)
    aq_l = _rope(_rms(aq_l, q_gain), cos, sin)
    ak_l = _rope(_rms(ak_l, k_gain), cos, sin)
    ak_c = _rms(ak_c, k_gain)
    bq_l = _rope(bq_l, cos, sin)
    bk_l = _rope(bk_l, cos, sin)
    lam = (jnp.exp(jnp.sum(lq1.astype(jnp.float32) * lk1.astype(jnp.float32)))
           - jnp.exp(jnp.sum(lq2.astype(jnp.float32) * lk2.astype(jnp.float32))) + lam_init)
    a_lat = _gqa_dense(aq_l, jnp.concatenate([ak_c, ak_l], axis=1), jnp.concatenate([av_c, av_l], axis=1))
    b_lat = _diff_attn(bq_l, jnp.concatenate([bk_c, bk_l], axis=1), jnp.concatenate([bv_c, bv_l], axis=1),
                       lam, sub_gain, lam_init)
    y_lat = jnp.concatenate([a_lat, b_lat], axis=-1) @ w_out
    y_ctx = None
    if ctx_out:
        a_ctx = _gqa_dense(_rms(aq_c, q_gain), ak_c, av_c)
        b_ctx = _diff_attn(bq_c, bk_c, bv_c, lam, sub_gain, lam_init)
        y_ctx = jnp.concatenate([a_ctx, b_ctx], axis=-1) @ w_out
    return y_lat, y_ctx


def _mixer_c(h_lat, h_ctx, w_in, w_out, sink, cos, sin, ctx_out):
    q_l, k_l, v_l = _split_c(h_lat @ w_in)
    q_c, k_c, v_c = _split_c(h_ctx @ w_in)
    q_l = _rope(q_l, cos, sin)
    k_l = _rope(k_l, cos, sin)
    y_lat = _window_attn(q_l, k_l, v_l, k_c, v_c, sink) @ w_out
    y_ctx = None
    if ctx_out:
        y_ctx = _gqa_dense(q_c, k_c, v_c, sink) @ w_out
    return y_lat, y_ctx


def setup_inputs(seed: int = 0) -> dict:
    key = jax.random.key(seed)
    ks = jax.random.split(key, 23)
    f32 = jnp.float32

    def nrm(k, shape, scale):
        return jax.random.normal(k, shape, f32) * scale

    def gain(k, shape):
        return 1.0 + nrm(k, shape, 0.02)

    D = D_MODEL
    return {
        'x': nrm(ks[0], (BATCH, SEQ, D), 1.0),
        'c': nrm(ks[1], (BATCH, D), 1.0),
        'ctx': nrm(ks[2], (BATCH, CTX_LEN, D), 1.0),
        'c_ctx': nrm(ks[3], (D,), 1.0),
        'w_mod': nrm(ks[4], (DEPTH, D, N_SUB * 3 * D), 0.5 * D ** -0.5),
        'b_mod': nrm(ks[5], (DEPTH, N_SUB * 3 * D), 0.02),
        'g_pre': gain(ks[6], (DEPTH, N_SUB, D)),
        'g_post': gain(ks[7], (DEPTH, N_SUB, D)),
        'w_ffn_gate': nrm(ks[8], (DEPTH, 2, D, D_FF), D ** -0.5),
        'w_ffn_up': nrm(ks[9], (DEPTH, 2, D, D_FF), D ** -0.5),
        'w_ffn_down': nrm(ks[10], (DEPTH, 2, D_FF, D), D_FF ** -0.5),
        'w_in_ab': nrm(ks[11], (N_EVEN, D, AB_IN), D ** -0.5),
        'w_out_ab': nrm(ks[12], (N_EVEN, AB_OUT, D), AB_OUT ** -0.5),
        'q_gain_a': gain(ks[13], (N_EVEN, HEAD_DIM)),
        'k_gain_a': gain(ks[14], (N_EVEN, HEAD_DIM)),
        'lam_q1': nrm(ks[15], (N_EVEN, HEAD_DIM), 0.1),
        'lam_k1': nrm(ks[16], (N_EVEN, HEAD_DIM), 0.1),
        'lam_q2': nrm(ks[17], (N_EVEN, HEAD_DIM), 0.1),
        'lam_k2': nrm(ks[18], (N_EVEN, HEAD_DIM), 0.1),
        'sub_gain_b': gain(ks[19], (N_EVEN, 2 * HEAD_DIM)),
        'w_in_c': nrm(ks[20], (N_ODD, D, C_IN), D ** -0.5),
        'w_out_c': nrm(ks[21], (N_ODD, C_OUT, D), C_OUT ** -0.5),
        'sink_c': nrm(ks[22], (N_ODD, C_HEADS), 0.5),
    }


def reference(x, c, ctx, c_ctx, w_mod, b_mod, g_pre, g_post, w_ffn_gate, w_ffn_up, w_ffn_down,
              w_in_ab, w_out_ab, q_gain_a, k_gain_a, lam_q1, lam_k1, lam_q2, lam_k2, sub_gain_b,
              w_in_c, w_out_c, sink_c):
    B, L, D = x.shape
    cos, sin = _axial_rope(L)
    x_lat, x_ctx = x, ctx
    for i in range(DEPTH):
        last = i == DEPTH - 1
        mod_l = (jax.nn.silu(c) @ w_mod[i] + b_mod[i]).reshape(B, N_SUB, 3, 1, D)
        mod_c = (jax.nn.silu(c_ctx) @ w_mod[i] + b_mod[i]).reshape(N_SUB, 3, D)

        wg, wu, wd = w_ffn_gate[i, 0], w_ffn_up[i, 0], w_ffn_down[i, 0]
        x_lat = _residual(x_lat, _swiglu(_pre(x_lat, g_pre[i, 0], mod_l[:, 0, 0], mod_l[:, 0, 1]), wg, wu, wd),
                          g_post[i, 0], mod_l[:, 0, 2], HALF_STEP)
        x_ctx = _residual(x_ctx, _swiglu(_pre(x_ctx, g_pre[i, 0], mod_c[0, 0], mod_c[0, 1]), wg, wu, wd),
                          g_post[i, 0], mod_c[0, 2], HALF_STEP)

        h_lat = _pre(x_lat, g_pre[i, 1], mod_l[:, 1, 0], mod_l[:, 1, 1])
        h_ctx = _pre(x_ctx, g_pre[i, 1], mod_c[1, 0], mod_c[1, 1])
        if i % 2 == 0:
            e = i // 2
            lam_init = 0.8 - 0.6 * math.exp(-0.3 * i)
            y_lat, y_ctx = _mixer_ab(h_lat, h_ctx, w_in_ab[e], w_out_ab[e], q_gain_a[e], k_gain_a[e],
                                     lam_q1[e], lam_k1[e], lam_q2[e], lam_k2[e], sub_gain_b[e], lam_init,
                                     cos, sin, not last)
        else:
            o = i // 2
            y_lat, y_ctx = _mixer_c(h_lat, h_ctx, w_in_c[o], w_out_c[o], sink_c[o], cos, sin, not last)
        x_lat = _residual(x_lat, y_lat, g_post[i, 1], mod_l[:, 1, 2], 1.0)
        if not last:
            x_ctx = _residual(x_ctx, y_ctx, g_post[i, 1], mod_c[1, 2], 1.0)

        wg, wu, wd = w_ffn_gate[i, 1], w_ffn_up[i, 1], w_ffn_down[i, 1]
        x_lat = _residual(x_lat, _swiglu(_pre(x_lat, g_pre[i, 2], mod_l[:, 2, 0], mod_l[:, 2, 1]), wg, wu, wd),
                          g_post[i, 2], mod_l[:, 2, 2], HALF_STEP)
        if not last:
            x_ctx = _residual(x_ctx, _swiglu(_pre(x_ctx, g_pre[i, 2], mod_c[2, 0], mod_c[2, 1]), wg, wu, wd),
                              g_post[i, 2], mod_c[2, 2], HALF_STEP)
    return x_lat
```

```python
import functools
import math

import numpy as np
import jax
import jax.numpy as jnp
from jax import lax
from jax.experimental import pallas as pl
from jax.experimental.pallas import tpu as pltpu

D_MODEL = 1024
CTX_LEN = 256
GRID_W = 64
HEAD_DIM = 64
ROPE_HALF = HEAD_DIM // 2
ROPE_THETA = 10000.0
A_HEADS = 8
A_KV_HEADS = 2
B_HEADS = 4
C_HEADS = 16
C_KV_HEADS = 4
WINDOW = 128
D_FF = 2816
N_SUB = 3
HALF_STEP = 0.5
EPS = 1e-6

LANES = 128
MOD_ROWS = 24
VMEM_LIMIT = 56 * 1024 * 1024

F32 = jnp.float32
BF16 = jnp.bfloat16
NT_DIMS = (((1,), (1,)), ((), ()))


def _params(n_grid):
    return pltpu.CompilerParams(dimension_semantics=("parallel",) * n_grid,
                                vmem_limit_bytes=VMEM_LIMIT)


def _rms_rows(x):
    return x * lax.rsqrt(jnp.mean(x * x, axis=-1, keepdims=True) + EPS)


def _pre_norm(x, mod_ref, gpre_ref, sub):
    shift = mod_ref[0, 3 * sub + 0:3 * sub + 1, :]
    scale = mod_ref[0, 3 * sub + 1:3 * sub + 2, :]
    return _rms_rows(x) * gpre_ref[sub:sub + 1, :] * (1.0 + scale) + shift


def _post_residual(x, y, mod_ref, gpost_ref, sub, res_w):
    gate = mod_ref[0, 3 * sub + 2:3 * sub + 3, :]
    return x + res_w * gate * (_rms_rows(y) * gpost_ref[sub:sub + 1, :])


def _mod_kernel(c_ref, w_ref, b_ref, o_ref):
    c = c_ref[...]
    act = (c * jax.nn.sigmoid(c)).astype(BF16)
    o_ref[0] = jnp.dot(act, w_ref[0].astype(BF16), preferred_element_type=F32) + b_ref[0]


def _modulation(cc, w_mod, b_mod):
    depth, d, n = w_mod.shape
    tn = 1152
    return pl.pallas_call(
        _mod_kernel,
        grid=(depth, n // tn),
        in_specs=[pl.BlockSpec((MOD_ROWS, d), lambda i, j: (0, 0)),
                  pl.BlockSpec((1, d, tn), lambda i, j: (i, 0, j)),
                  pl.BlockSpec((1, 1, tn), lambda i, j: (i, 0, j))],
        out_specs=pl.BlockSpec((1, MOD_ROWS, tn), lambda i, j: (i, 0, j)),
        out_shape=jax.ShapeDtypeStruct((depth, MOD_ROWS, n), F32),
        compiler_params=_params(2),
        name="adaln_mod",
    )(cc, w_mod, b_mod.reshape(depth, 1, n))


def _ffn_kernel(x_ref, mod_ref, gpre_ref, gpost_ref, wg_ref, wu_ref, wd_ref, o_ref, *, sub):
    x = x_ref[0]
    hb = _pre_norm(x, mod_ref, gpre_ref, sub).astype(BF16)
    g = jnp.dot(hb, wg_ref[...], preferred_element_type=F32)
    u = jnp.dot(hb, wu_ref[...], preferred_element_type=F32)
    a = (g * jax.nn.sigmoid(g) * u).astype(BF16)
    y = jnp.dot(a, wd_ref[...], preferred_element_type=F32)
    o_ref[0] = _post_residual(x, y, mod_ref, gpost_ref, sub, HALF_STEP)


def _const_spec(shape):
    return pl.BlockSpec(shape, lambda *_: (0,) * len(shape), pipeline_mode=pl.Buffered(1))


def _ffn(x, mod, mod_row, gpre, gpost, wg, wu, wd, sub, tm):
    nb, t, d = x.shape
    kern = functools.partial(_ffn_kernel, sub=sub)
    return pl.pallas_call(
        kern,
        grid=(nb, t // tm),
        in_specs=[pl.BlockSpec((1, tm, d), lambda b, i: (b, i, 0)),
                  pl.BlockSpec((1, 3 * N_SUB, d), lambda b, i: (mod_row(b), 0, 0)),
                  _const_spec(gpre.shape), _const_spec(gpost.shape),
                  _const_spec(wg.shape), _const_spec(wu.shape), _const_spec(wd.shape)],
        out_specs=pl.BlockSpec((1, tm, d), lambda b, i: (b, i, 0)),
        out_shape=jax.ShapeDtypeStruct(x.shape, F32),
        compiler_params=_params(2),
        name="swiglu_sublayer",
    )(x, mod, gpre, gpost, wg, wu, wd)


def _proj_kernel(x_ref, mod_ref, gpre_ref, w_ref, cos_ref, sin_ref, gain_ref, o_ref, *, blocks):
    x = x_ref[0]
    tm = x.shape[0]
    hb = _pre_norm(x, mod_ref, gpre_ref, 1).astype(BF16)
    r = jnp.dot(hb, w_ref[...], preferred_element_type=F32)
    lane = lax.broadcasted_iota(jnp.int32, (tm, LANES), 1)
    lo = lane < HEAD_DIM
    first_half = (lane & (HEAD_DIM - 1)) < ROPE_HALF
    for j, (gain_row, rotary, scale) in enumerate(blocks):
        v = r[:, LANES * j:LANES * (j + 1)]
        if gain_row is not None:
            v2 = v * v
            ms_lo = jnp.sum(jnp.where(lo, v2, 0.0), axis=-1, keepdims=True)
            ms_hi = jnp.sum(jnp.where(lo, 0.0, v2), axis=-1, keepdims=True)
            ms = jnp.where(lo, ms_lo, ms_hi) * (1.0 / HEAD_DIM)
            v = v * lax.rsqrt(ms + EPS) * gain_ref[gain_row:gain_row + 1, :]
        if rotary:
            partner = jnp.where(first_half, pltpu.roll(v, LANES - ROPE_HALF, 1),
                                pltpu.roll(v, ROPE_HALF, 1))
            v = v * cos_ref[...] + partner * sin_ref[...]
        if scale != 1.0:
            v = v * scale
        o_ref[0, :, LANES * j:LANES * (j + 1)] = v.astype(BF16)


def _proj(x, mod, mod_row, gpre, w, cos, sin, gains, blocks, tm, rope_rows):
    nb, t, d = x.shape
    n = w.shape[1]
    kern = functools.partial(_proj_kernel, blocks=tuple(blocks))
    rope_map = (lambda b, i: (i, 0)) if rope_rows else (lambda b, i: (0, 0))
    return pl.pallas_call(
        kern,
        grid=(nb, t // tm),
        in_specs=[pl.BlockSpec((1, tm, d), lambda b, i: (b, i, 0)),
                  pl.BlockSpec((1, 3 * N_SUB, d), lambda b, i: (mod_row(b), 0, 0)),
                  _const_spec(gpre.shape), _const_spec(w.shape),
                  pl.BlockSpec((tm, LANES), rope_map),
                  pl.BlockSpec((tm, LANES), rope_map),
                  _const_spec(gains.shape)],
        out_specs=pl.BlockSpec((1, tm, n), lambda b, i: (b, i, 0)),
        out_shape=jax.ShapeDtypeStruct((nb, t, n), BF16),
        compiler_params=_params(2),
        name="qkv_proj",
    )(x, mod, gpre, w, cos, sin, gains)


def _split_heads(q):
    lane = lax.broadcasted_iota(jnp.int32, q.shape, 1)
    zero = jnp.zeros_like(q)
    return jnp.concatenate([jnp.where(lane < HEAD_DIM, q, zero),
                            jnp.where(lane < HEAD_DIM, zero, q)], axis=0)


def _scores(qq, keys):
    return [lax.dot_general(qq, k, NT_DIMS, preferred_element_type=F32) for k in keys]


def _row_max(parts):
    return functools.reduce(jnp.maximum, [jnp.max(s, axis=-1, keepdims=True) for s in parts])


def _row_sum(parts):
    return functools.reduce(lambda a, b: a + b, [jnp.sum(e, axis=-1, keepdims=True) for e in parts])


def _pv(probs, values):
    return functools.reduce(
        lambda a, b: a + b,
        [jnp.dot(p.astype(BF16), v, preferred_element_type=F32) for p, v in zip(probs, values)])


def _merge_heads(o2, tq):
    lane = lax.broadcasted_iota(jnp.int32, (tq, LANES), 1)
    return jnp.where(lane < HEAD_DIM, o2[:tq], o2[tq:])


def _gqa_kernel(*refs, n_kv):
    q_ref, kv_refs, o_ref = refs[0], refs[1:1 + 2 * n_kv], refs[-1]
    keys = [r[0] for r in kv_refs[:n_kv]]
    vals = [r[0] for r in kv_refs[n_kv:]]
    tq = q_ref.shape[1]
    s = _scores(_split_heads(q_ref[0]), keys)
    m = _row_max(s)
    e = [jnp.exp(x - m) for x in s]
    o2 = _pv(e, vals) / _row_sum(e)
    o_ref[0] = _merge_heads(o2, tq).astype(BF16)


def _diff_kernel(*refs, n_kv, lam_init):
    q_ref, kv_refs = refs[0], refs[1:1 + 2 * n_kv]
    lam_ref, gain_ref, o_ref = refs[-3], refs[-2], refs[-1]
    keys = [r[0] for r in kv_refs[:n_kv]]
    vals = [r[0] for r in kv_refs[n_kv:]]
    tq = q_ref.shape[1]
    lv = lam_ref[...]
    lam = (jnp.exp(jnp.sum(lv[0:1] * lv[1:2], axis=-1, keepdims=True))
           - jnp.exp(jnp.sum(lv[2:3] * lv[3:4], axis=-1, keepdims=True)) + lam_init)
    s = _scores(_split_heads(q_ref[0]), keys)
    m = _row_max(s)
    e = [jnp.exp(x - m) for x in s]
    inv = 1.0 / _row_sum(e)
    w1 = inv[:tq]
    w2 = lam * inv[tq:]
    a = [x[:tq] * w1 - x[tq:] * w2 for x in e]
    o = _pv(a, vals)
    o = _rms_rows(o) * gain_ref[0:1, :] * (1.0 - lam_init)
    o_ref[0] = o.astype(BF16)


def _window_kernel(q_ref, kc_ref, vc_ref, kl_ref, vl_ref, bias_ref, sink_ref, o_ref, *, band):
    tq = q_ref.shape[1]
    seq = kl_ref.shape[1]
    i = pl.program_id(2)
    start = jnp.clip(i * tq - WINDOW, 0, seq - band)
    start = pl.multiple_of(start, LANES)
    keys = [kc_ref[0], kl_ref[0, pl.ds(start, band), :]]
    vals = [vc_ref[0], vl_ref[0, pl.ds(start, band), :]]
    bias = bias_ref[0]
    for t in range(q_ref.shape[2] // LANES):
        s = _scores(_split_heads(q_ref[0, :, LANES * t:LANES * (t + 1)]), keys)
        s[1] = s[1] + bias
        srow = sink_ref[0, t:t + 1, :]
        sk = jnp.concatenate([jnp.broadcast_to(srow[:, 0:1], (tq, 1)),
                              jnp.broadcast_to(srow[:, HEAD_DIM:HEAD_DIM + 1], (tq, 1))], axis=0)
        m = jnp.maximum(_row_max(s), sk)
        e = [jnp.exp(x - m) for x in s]
        o2 = _pv(e, vals) / (_row_sum(e) + jnp.exp(sk - m))
        o_ref[0, :, LANES * t:LANES * (t + 1)] = _merge_heads(o2, tq).astype(BF16)


def _col_spec(rows, col):
    if callable(col):
        return pl.BlockSpec((1, rows, LANES), lambda b, j, i: (b, 0, col(j)))
    return pl.BlockSpec((1, rows, LANES), lambda b, j, i: (b, 0, col))


def _attn_ab(kind, q_arr, kv_arrs, q_col, k_col, v_col, tq, extra=(), lam_init=0.0):
    nb, t, _ = q_arr.shape
    n_kv = len(kv_arrs)
    n_blocks = 4
    in_specs = [pl.BlockSpec((1, tq, LANES), lambda b, j, i: (b, i, q_col(j)))]
    in_specs += [_col_spec(a.shape[1], k_col) for a in kv_arrs]
    in_specs += [_col_spec(a.shape[1], v_col) for a in kv_arrs]
    in_specs += [pl.BlockSpec(e.shape, lambda b, j, i: (0, 0)) for e in extra]
    if kind == "gqa":
        kern = functools.partial(_gqa_kernel, n_kv=n_kv)
    else:
        kern = functools.partial(_diff_kernel, n_kv=n_kv, lam_init=lam_init)
    return pl.pallas_call(
        kern,
        grid=(nb, n_blocks, t // tq),
        in_specs=in_specs,
        out_specs=pl.BlockSpec((1, tq, LANES), lambda b, j, i: (b, i, j)),
        out_shape=jax.ShapeDtypeStruct((nb, t, n_blocks * LANES), BF16),
        compiler_params=_params(3),
        name="attn_" + kind,
    )(q_arr, *kv_arrs, *kv_arrs, *extra)


def _attn_window(p_lat, p_ctx, bias, sinkv, tq, band):
    nb, t, _ = p_lat.shape
    n_q = t // tq
    qw = 4 * LANES
    case = lambda i: jnp.where(i == 0, 0, jnp.where(i == n_q - 1, 2, 1))
    return pl.pallas_call(
        functools.partial(_window_kernel, band=band),
        grid=(nb, 2, n_q),
        in_specs=[pl.BlockSpec((1, tq, qw), lambda b, j, i: (b, i, j)),
                  pl.BlockSpec((1, CTX_LEN, LANES), lambda b, j, i: (b, 0, j)),
                  pl.BlockSpec((1, CTX_LEN, LANES), lambda b, j, i: (b, 0, 2 + j)),
                  pl.BlockSpec((1, t, LANES), lambda b, j, i: (b, 0, 8 + j)),
                  pl.BlockSpec((1, t, LANES), lambda b, j, i: (b, 0, 10 + j)),
                  pl.BlockSpec((1, 2 * tq, band), lambda b, j, i: (case(i), 0, 0)),
                  pl.BlockSpec((1, 4, LANES), lambda b, j, i: (j, 0, 0))],
        out_specs=pl.BlockSpec((1, tq, qw), lambda b, j, i: (b, i, j)),
        out_shape=jax.ShapeDtypeStruct((nb, t, 2 * qw), BF16),
        compiler_params=_params(3),
        name="attn_window",
    )(p_lat, p_ctx, p_ctx, p_lat, p_lat, bias, sinkv)


def _outproj_kernel(*refs, n_in):
    a_refs = refs[:n_in]
    x_ref, mod_ref, gpost_ref, w_ref, o_ref = refs[n_in:]
    y = None
    row = 0
    for a_ref in a_refs:
        k = a_ref.shape[2]
        part = jnp.dot(a_ref[0], w_ref[row:row + k, :], preferred_element_type=F32)
        y = part if y is None else y + part
        row += k
    o_ref[0] = _post_residual(x_ref[0], y, mod_ref, gpost_ref, 1, 1.0)


def _outproj(a_list, x, mod, mod_row, gpost, w, tm):
    nb, t, d = x.shape
    in_specs = [pl.BlockSpec((1, tm, a.shape[2]), lambda b, i: (b, i, 0)) for a in a_list]
    in_specs += [pl.BlockSpec((1, tm, d), lambda b, i: (b, i, 0)),
                 pl.BlockSpec((1, 3 * N_SUB, d), lambda b, i: (mod_row(b), 0, 0)),
                 _const_spec(gpost.shape), _const_spec(w.shape)]
    return pl.pallas_call(
        functools.partial(_outproj_kernel, n_in=len(a_list)),
        grid=(nb, t // tm),
        in_specs=in_specs,
        out_specs=pl.BlockSpec((1, tm, d), lambda b, i: (b, i, 0)),
        out_shape=jax.ShapeDtypeStruct(x.shape, F32),
        compiler_params=_params(2),
        name="out_proj",
    )(*a_list, x, mod, gpost, w)


def _rope_tables(seq):
    rows = seq // GRID_W
    row = np.repeat(np.arange(rows, dtype=np.float64), GRID_W)
    col = np.tile(np.arange(GRID_W, dtype=np.float64), rows)
    inv = ROPE_THETA ** (-np.arange(0, ROPE_HALF, 2, dtype=np.float64) / ROPE_HALF)
    ang = np.concatenate([row[:, None] * inv, col[:, None] * inv], axis=-1)
    cos, sin = np.cos(ang), np.sin(ang)
    cos128 = np.tile(cos, (1, LANES // ROPE_HALF))
    sin128 = np.tile(np.concatenate([-sin, sin], axis=-1), (1, LANES // HEAD_DIM))
    return jnp.asarray(cos128, F32), jnp.asarray(sin128, F32)


def _window_bias(seq, tq, band):
    n_q = seq // tq
    out = np.zeros((3, 2 * tq, band), np.float32)
    for case, i in enumerate((0, 1, n_q - 1)):
        start = min(max(i * tq - WINDOW, 0), seq - band)
        qpos = i * tq + np.arange(tq)[:, None]
        kpos = start + np.arange(band)[None, :]
        m = np.where(np.abs(kpos - qpos) <= WINDOW, 0.0, -np.inf).astype(np.float32)
        out[case] = np.concatenate([m, m], axis=0)
    return jnp.asarray(out)


def _pair_order(n_heads, n_kv):
    g = n_heads // n_kv
    order = []
    for kv in range(0, n_kv, 2):
        for t in range(g):
            order += [kv * g + t, (kv + 1) * g + t]
    return order


def _permute_heads(w, order, axis):
    shape = w.shape
    n = len(order)
    if axis == 1:
        return w.reshape(shape[0], n, HEAD_DIM)[:, np.asarray(order), :].reshape(shape)
    return w.reshape(n, HEAD_DIM, shape[1])[np.asarray(order)].reshape(shape)


def _tile_lanes(v):
    return jnp.tile(v, LANES // v.shape[-1])


def _pad_rows(rows, n=8):
    m = jnp.stack([_tile_lanes(r) if r.shape[-1] != LANES else r for r in rows]).astype(F32)
    return jnp.concatenate([m, jnp.zeros((n - m.shape[0], LANES), F32)], axis=0)


def kernel(x, c, ctx, c_ctx, w_mod, b_mod, g_pre, g_post, w_ffn_gate, w_ffn_up, w_ffn_down,
           w_in_ab, w_out_ab, q_gain_a, k_gain_a, lam_q1, lam_k1, lam_q2, lam_k2, sub_gain_b,
           w_in_c, w_out_c, sink_c):
    nb, seq, d = x.shape
    depth = w_mod.shape[0]
    assert depth == 2, "an odd layer that is not last would also need the windowed mixer's context output"
    n_ctx = ctx.shape[1]
    ctx_row = nb

    cc = jnp.concatenate([c, c_ctx[None, :], jnp.zeros((MOD_ROWS - nb - 1, d), F32)], axis=0)
    mod_all = _modulation(cc, w_mod, b_mod).reshape(depth, MOD_ROWS, 3 * N_SUB, d)

    cos_l, sin_l = _rope_tables(seq)
    scale = HEAD_DIM ** -0.5

    lat_row = lambda b: b
    ctx_rowf = lambda b: ctx_row
    tm_lat = 512
    tm_ctx = 512
    cos_c = jnp.ones((tm_ctx, LANES), F32)
    sin_c = jnp.zeros((tm_ctx, LANES), F32)
    x_lat = x
    x_ctx = ctx.reshape(1, nb * n_ctx, d)

    for i in range(depth):
        last = i == depth - 1
        mod = mod_all[i]
        gpre, gpost = g_pre[i], g_post[i]
        wg, wu, wd = (w_ffn_gate[i].astype(BF16), w_ffn_up[i].astype(BF16), w_ffn_down[i].astype(BF16))

        x_lat = _ffn(x_lat, mod, lat_row, gpre, gpost, wg[0], wu[0], wd[0], 0, tm_lat)
        x_ctx = _ffn(x_ctx, mod, ctx_rowf, gpre, gpost, wg[0], wu[0], wd[0], 0, tm_ctx)

        if i % 2 == 0:
            e = i // 2
            lam_init = 0.8 - 0.6 * math.exp(-0.3 * i)
            order_a = _pair_order(A_HEADS, A_KV_HEADS)
            n_qa = A_HEADS * HEAD_DIM
            w_in = w_in_ab[e]
            w_in = jnp.concatenate([_permute_heads(w_in[:, :n_qa], order_a, 1), w_in[:, n_qa:]],
                                   axis=1).astype(BF16)
            w_out = w_out_ab[e]
            w_out = jnp.concatenate([_permute_heads(w_out[:n_qa], order_a, 0), w_out[n_qa:]],
                                    axis=0).astype(BF16)
            gains = _pad_rows([q_gain_a[e], k_gain_a[e]])
            lamv = _pad_rows([jnp.pad(v, (0, LANES - HEAD_DIM))
                              for v in (lam_q1[e], lam_k1[e], lam_q2[e], lam_k2[e])])
            subg = _pad_rows([sub_gain_b[e]])
            blocks = ([(0, True, scale)] * 4 + [(1, True, 1.0), (None, False, 1.0)]
                      + [(None, True, scale)] * 4 + [(None, True, 1.0)] * 4 + [(None, False, 1.0)] * 4)
            blocks_ctx = [(g, False, s) for g, _, s in blocks]
            p_lat = _proj(x_lat, mod, lat_row, gpre, w_in, cos_l, sin_l, gains, blocks, tm_lat, True)
            p_ctx = _proj(x_ctx, mod, ctx_rowf, gpre, w_in, cos_c, sin_c, gains, blocks_ctx, tm_ctx,
                          False).reshape(nb, n_ctx, -1)
            qa, qb = (lambda j: j), (lambda j: 6 + j)
            kb, vb = (lambda j: 10 + j), (lambda j: 14 + j)
            a_lat = _attn_ab("gqa", p_lat, [p_ctx, p_lat], qa, 4, 5, 256)
            b_lat = _attn_ab("diff", p_lat, [p_ctx, p_lat], qb, kb, vb, 256, (lamv, subg), lam_init)
            x_lat = _outproj([a_lat, b_lat], x_lat, mod, lat_row, gpost, w_out, tm_lat)
            if not last:
                a_ctx = _attn_ab("gqa", p_ctx, [p_ctx], qa, 4, 5, n_ctx)
                b_ctx = _attn_ab("diff", p_ctx, [p_ctx], qb, kb, vb, n_ctx, (lamv, subg), lam_init)
                x_ctx = _outproj([a_ctx.reshape(1, nb * n_ctx, -1), b_ctx.reshape(1, nb * n_ctx, -1)],
                                 x_ctx, mod, ctx_rowf, gpost, w_out, tm_ctx)
        else:
            o = i // 2
            order_c = _pair_order(C_HEADS, C_KV_HEADS)
            n_qc = C_HEADS * HEAD_DIM
            w_in = w_in_c[o]
            w_in_lat = jnp.concatenate([_permute_heads(w_in[:, :n_qc], order_c, 1), w_in[:, n_qc:]],
                                       axis=1).astype(BF16)
            w_in_ctx = w_in[:, n_qc:].astype(BF16)
            w_out = _permute_heads(w_out_c[o], order_c, 0).astype(BF16)
            sink = sink_c[o][np.asarray(order_c)].astype(F32)
            sinkv = jnp.repeat(sink, HEAD_DIM).reshape(2, 4, LANES)
            gains = jnp.zeros((8, LANES), F32)
            blocks = [(None, True, scale)] * 8 + [(None, True, 1.0)] * 2 + [(None, False, 1.0)] * 2
            blocks_ctx = [(None, False, 1.0)] * 4
            p_lat = _proj(x_lat, mod, lat_row, gpre, w_in_lat, cos_l, sin_l, gains, blocks, tm_lat, True)
            p_ctx = _proj(x_ctx, mod, ctx_rowf, gpre, w_in_ctx, cos_c, sin_c, gains, blocks_ctx, tm_ctx,
                          False).reshape(nb, n_ctx, -1)
            tq = 256
            band = tq + 2 * WINDOW
            c_lat = _attn_window(p_lat, p_ctx, _window_bias(seq, tq, band), sinkv, tq, band)
            x_lat = _outproj([c_lat], x_lat, mod, lat_row, gpost, w_out, tm_lat)

        x_lat = _ffn(x_lat, mod, lat_row, gpre, gpost, wg[1], wu[1], wd[1], 2, tm_lat)
        if not last:
            x_ctx = _ffn(x_ctx, mod, ctx_rowf, gpre, gpost, wg[1], wu[1], wd[1], 2, tm_ctx)
    return x_lat
```

```python
import functools
import math

import numpy as np
import jax
import jax.numpy as jnp
from jax import lax
from jax.experimental import pallas as pl
from jax.experimental.pallas import tpu as pltpu

D_MODEL = 1024
CTX_LEN = 256
GRID_W = 64
HEAD_DIM = 64
ROPE_HALF = HEAD_DIM // 2
ROPE_THETA = 10000.0
A_HEADS = 8
A_KV_HEADS = 2
B_HEADS = 4
C_HEADS = 16
C_KV_HEADS = 4
WINDOW = 128
D_FF = 2816
N_SUB = 3
HALF_STEP = 0.5
EPS = 1e-6

LANES = 128
MOD_ROWS = 24
VMEM_LIMIT = 56 * 1024 * 1024
KEY_CHUNK = 1024
LOG2E = 1.4426950408889634

F32 = jnp.float32
BF16 = jnp.bfloat16
NT_DIMS = (((1,), (1,)), ((), ()))


def _params(n_grid):
    return pltpu.CompilerParams(dimension_semantics=("parallel",) * n_grid,
                                vmem_limit_bytes=VMEM_LIMIT)


def _rms_rows(x):
    return x * lax.rsqrt(jnp.mean(x * x, axis=-1, keepdims=True) + EPS)


def _pre_norm(x, mod_ref, gpre_ref, sub):
    shift = mod_ref[0, 3 * sub + 0:3 * sub + 1, :]
    scale = mod_ref[0, 3 * sub + 1:3 * sub + 2, :]
    return _rms_rows(x) * gpre_ref[sub:sub + 1, :] * (1.0 + scale) + shift


def _post_residual(x, y, mod_ref, gpost_ref, sub, res_w):
    gate = mod_ref[0, 3 * sub + 2:3 * sub + 3, :]
    return x + res_w * gate * (_rms_rows(y) * gpost_ref[sub:sub + 1, :])


def _mod_kernel(c_ref, w_ref, b_ref, o_ref):
    c = c_ref[...]
    act = (c * jax.nn.sigmoid(c)).astype(BF16)
    o_ref[0] = jnp.dot(act, w_ref[0].astype(BF16), preferred_element_type=F32) + b_ref[0]


def _modulation(cc, w_mod, b_mod):
    depth, d, n = w_mod.shape
    tn = 1152
    return pl.pallas_call(
        _mod_kernel,
        grid=(depth, n // tn),
        in_specs=[pl.BlockSpec((MOD_ROWS, d), lambda i, j: (0, 0)),
                  pl.BlockSpec((1, d, tn), lambda i, j: (i, 0, j)),
                  pl.BlockSpec((1, 1, tn), lambda i, j: (i, 0, j))],
        out_specs=pl.BlockSpec((1, MOD_ROWS, tn), lambda i, j: (i, 0, j)),
        out_shape=jax.ShapeDtypeStruct((depth, MOD_ROWS, n), F32),
        compiler_params=_params(2),
        name="adaln_mod",
    )(cc, w_mod, b_mod.reshape(depth, 1, n))


def _ffn_kernel(x_ref, mod_ref, gpre_ref, gpost_ref, wg_ref, wu_ref, wd_ref, o_ref, *, sub):
    x = x_ref[0]
    hb = _pre_norm(x, mod_ref, gpre_ref, sub).astype(BF16)
    g = jnp.dot(hb, wg_ref[...], preferred_element_type=F32)
    u = jnp.dot(hb, wu_ref[...], preferred_element_type=F32)
    a = (g * jax.nn.sigmoid(g) * u).astype(BF16)
    y = jnp.dot(a, wd_ref[...], preferred_element_type=F32)
    o_ref[0] = _post_residual(x, y, mod_ref, gpost_ref, sub, HALF_STEP)


def _const_spec(shape):
    return pl.BlockSpec(shape, lambda *_: (0,) * len(shape), pipeline_mode=pl.Buffered(1))


def _ffn(x, mod, mod_row, gpre, gpost, wg, wu, wd, sub, tm):
    nb, t, d = x.shape
    kern = functools.partial(_ffn_kernel, sub=sub)
    return pl.pallas_call(
        kern,
        grid=(nb, t // tm),
        in_specs=[pl.BlockSpec((1, tm, d), lambda b, i: (b, i, 0)),
                  pl.BlockSpec((1, 3 * N_SUB, d), lambda b, i: (mod_row(b), 0, 0)),
                  _const_spec(gpre.shape), _const_spec(gpost.shape),
                  _const_spec(wg.shape), _const_spec(wu.shape), _const_spec(wd.shape)],
        out_specs=pl.BlockSpec((1, tm, d), lambda b, i: (b, i, 0)),
        out_shape=jax.ShapeDtypeStruct(x.shape, F32),
        compiler_params=_params(2),
        name="swiglu_sublayer",
    )(x, mod, gpre, gpost, wg, wu, wd)


def _proj_kernel(x_ref, mod_ref, gpre_ref, w_ref, cos_ref, sin_ref, gain_ref, o_ref, *, blocks):
    x = x_ref[0]
    tm = x.shape[0]
    hb = _pre_norm(x, mod_ref, gpre_ref, 1).astype(BF16)
    r = jnp.dot(hb, w_ref[...], preferred_element_type=F32)
    lane = lax.broadcasted_iota(jnp.int32, (tm, LANES), 1)
    lo = lane < HEAD_DIM
    first_half = (lane & (HEAD_DIM - 1)) < ROPE_HALF
    for j, (gain_row, rotary, scale) in enumerate(blocks):
        v = r[:, LANES * j:LANES * (j + 1)]
        if gain_row is not None:
            v2 = v * v
            ms_lo = jnp.sum(jnp.where(lo, v2, 0.0), axis=-1, keepdims=True)
            ms_hi = jnp.sum(jnp.where(lo, 0.0, v2), axis=-1, keepdims=True)
            ms = jnp.where(lo, ms_lo, ms_hi) * (1.0 / HEAD_DIM)
            v = v * lax.rsqrt(ms + EPS) * gain_ref[gain_row:gain_row + 1, :]
        if rotary:
            partner = jnp.where(first_half, pltpu.roll(v, LANES - ROPE_HALF, 1),
                                pltpu.roll(v, ROPE_HALF, 1))
            v = v * cos_ref[...] + partner * sin_ref[...]
        if scale != 1.0:
            v = v * scale
        o_ref[0, :, LANES * j:LANES * (j + 1)] = v.astype(BF16)


def _proj(x, mod, mod_row, gpre, w, cos, sin, gains, blocks, tm, rope_rows):
    nb, t, d = x.shape
    n = w.shape[1]
    kern = functools.partial(_proj_kernel, blocks=tuple(blocks))
    rope_map = (lambda b, i: (i, 0)) if rope_rows else (lambda b, i: (0, 0))
    return pl.pallas_call(
        kern,
        grid=(nb, t // tm),
        in_specs=[pl.BlockSpec((1, tm, d), lambda b, i: (b, i, 0)),
                  pl.BlockSpec((1, 3 * N_SUB, d), lambda b, i: (mod_row(b), 0, 0)),
                  _const_spec(gpre.shape), _const_spec(w.shape),
                  pl.BlockSpec((tm, LANES), rope_map),
                  pl.BlockSpec((tm, LANES), rope_map),
                  _const_spec(gains.shape)],
        out_specs=pl.BlockSpec((1, tm, n), lambda b, i: (b, i, 0)),
        out_shape=jax.ShapeDtypeStruct((nb, t, n), BF16),
        compiler_params=_params(2),
        name="qkv_proj",
    )(x, mod, gpre, w, cos, sin, gains)


def _split_heads(q):
    lane = lax.broadcasted_iota(jnp.int32, q.shape, 1)
    zero = jnp.zeros_like(q)
    return jnp.concatenate([jnp.where(lane < HEAD_DIM, q, zero),
                            jnp.where(lane < HEAD_DIM, zero, q)], axis=0)


def _kv_chunks(kv_refs, n_kv):
    chunks = []
    for k_ref, v_ref in zip(kv_refs[:n_kv], kv_refs[n_kv:]):
        n = k_ref.shape[1]
        step = min(n, KEY_CHUNK)
        chunks += [(k_ref, v_ref, pl.ds(a, step), None) for a in range(0, n, step)]
    return chunks


def _streaming_attention(problems):
    items = [(pi, ci) for pi, (_, chunks) in enumerate(problems) for ci in range(len(chunks))]
    state = [[None, None] for _ in problems]
    scores, probs = {}, {}

    def stage_scores(w):
        pi, ci = items[w]
        qq, chunks = problems[pi]
        k_ref, _, rows, bias_ref = chunks[ci]
        s = lax.dot_general(qq, k_ref[0, rows, :], NT_DIMS, preferred_element_type=F32)
        scores[w] = s if bias_ref is None else s + bias_ref[0]

    def stage_exp(w):
        pi, _ = items[w]
        m = state[pi][0]
        s = scores.pop(w)
        s_max = jnp.max(s, axis=-1, keepdims=True)
        m_new = s_max if m is None else jnp.maximum(m, s_max)
        alpha = None if m is None else jnp.exp2(m - m_new)
        state[pi][0] = m_new
        probs[w] = (jnp.exp2(s - m_new).astype(BF16), alpha)

    def stage_values(w):
        pi, ci = items[w]
        _, v_ref, rows, _ = problems[pi][1][ci]
        p, alpha = probs.pop(w)
        v = v_ref[0, rows, :]
        pv = jnp.dot(p, jnp.concatenate([v, jnp.ones_like(v)], axis=1), preferred_element_type=F32)
        state[pi][1] = pv if alpha is None else alpha * state[pi][1] + pv

    n = len(items)
    for step in range(n + 2):
        if step < n:
            stage_scores(step)
        if 0 <= step - 1 < n:
            stage_exp(step - 1)
        if 0 <= step - 2 < n:
            stage_values(step - 2)
    return [(acc[:, :LANES], acc[:, LANES:], m) for m, acc in state]


def _merge_heads(o2, tq):
    lane = lax.broadcasted_iota(jnp.int32, (tq, LANES), 1)
    return jnp.where(lane < HEAD_DIM, o2[:tq], o2[tq:])


def _gqa_kernel(*refs, n_kv):
    q_ref, kv_refs, o_ref = refs[0], refs[1:1 + 2 * n_kv], refs[-1]
    tq = q_ref.shape[1]
    chunks = _kv_chunks(kv_refs, n_kv)
    cols = [slice(LANES * t, LANES * (t + 1)) for t in range(q_ref.shape[2] // LANES)]
    results = _streaming_attention([(_split_heads(q_ref[0, :, c]), chunks) for c in cols])
    for c, (acc, l, _) in zip(cols, results):
        o_ref[0, :, c] = _merge_heads(acc / l, tq).astype(BF16)


def _diff_kernel(*refs, n_kv, lam_init):
    q_ref, kv_refs = refs[0], refs[1:1 + 2 * n_kv]
    lam_ref, gain_ref, o_ref = refs[-3], refs[-2], refs[-1]
    tq = q_ref.shape[1]
    lv = lam_ref[...]
    lam = (jnp.exp(jnp.sum(lv[0:1] * lv[1:2], axis=-1, keepdims=True))
           - jnp.exp(jnp.sum(lv[2:3] * lv[3:4], axis=-1, keepdims=True)) + lam_init)
    (acc, l, _), = _streaming_attention([(_split_heads(q_ref[0]), _kv_chunks(kv_refs, n_kv))])
    o2 = acc / l
    o = o2[:tq] - lam * o2[tq:]
    o = _rms_rows(o) * gain_ref[0:1, :] * (1.0 - lam_init)
    o_ref[0] = o.astype(BF16)


def _window_kernel(q_ref, kc_ref, vc_ref, kl_ref, vl_ref, bias_ref, sink_ref, o_ref, *, band):
    tq = q_ref.shape[1]
    seq = kl_ref.shape[1]
    n_t = q_ref.shape[2] // LANES
    i = pl.program_id(2)
    start = pl.multiple_of(jnp.clip(i * tq - WINDOW, 0, seq - band), LANES)
    chunks = [(kc_ref, vc_ref, pl.ds(0, kc_ref.shape[1]), None),
              (kl_ref, vl_ref, pl.ds(start, band), bias_ref)]
    problems = [(_split_heads(q_ref[0, :, LANES * t:LANES * (t + 1)]), chunks) for t in range(n_t)]
    lane = lax.broadcasted_iota(jnp.int32, (tq, LANES), 1)
    for t, (acc, l, m) in enumerate(_streaming_attention(problems)):
        head = 2 * (pl.program_id(1) * n_t + t)
        l_lo = l[:tq] + jnp.exp2(sink_ref[pl.ds(head, 1), :] * LOG2E - m[:tq])
        l_hi = l[tq:] + jnp.exp2(sink_ref[pl.ds(head + 1, 1), :] * LOG2E - m[tq:])
        o = jnp.where(lane < HEAD_DIM, acc[:tq] / l_lo, acc[tq:] / l_hi)
        o_ref[0, :, LANES * t:LANES * (t + 1)] = o.astype(BF16)


def _col_spec(rows, col):
    if callable(col):
        return pl.BlockSpec((1, rows, LANES), lambda b, j, i: (b, 0, col(j)))
    return pl.BlockSpec((1, rows, LANES), lambda b, j, i: (b, 0, col))


def _attn_ab(kind, q_arr, kv_arrs, q_col, k_col, v_col, tq, extra=(), lam_init=0.0, n_t=1):
    nb, t, _ = q_arr.shape
    n_kv = len(kv_arrs)
    qw = n_t * LANES
    n_blocks = 4 // n_t
    in_specs = [pl.BlockSpec((1, tq, qw), lambda b, j, i: (b, i, q_col(j)))]
    in_specs += [_col_spec(a.shape[1], k_col) for a in kv_arrs]
    in_specs += [_col_spec(a.shape[1], v_col) for a in kv_arrs]
    in_specs += [pl.BlockSpec(e.shape, lambda b, j, i: (0, 0)) for e in extra]
    if kind == "gqa":
        kern = functools.partial(_gqa_kernel, n_kv=n_kv)
    else:
        kern = functools.partial(_diff_kernel, n_kv=n_kv, lam_init=lam_init)
    return pl.pallas_call(
        kern,
        grid=(nb, n_blocks, t // tq),
        in_specs=in_specs,
        out_specs=pl.BlockSpec((1, tq, qw), lambda b, j, i: (b, i, j)),
        out_shape=jax.ShapeDtypeStruct((nb, t, n_blocks * qw), BF16),
        compiler_params=_params(3),
        name="attn_" + kind,
    )(q_arr, *kv_arrs, *kv_arrs, *extra)


def _attn_window(p_lat, p_ctx, bias, sinkv, tq, band, n_t):
    nb, t, _ = p_lat.shape
    n_q = t // tq
    n_qblocks = sinkv.shape[0] // 2
    qw = n_t * LANES
    kcol = lambda j: (j * n_t) // 4
    case = lambda i: jnp.where(i == 0, 0, jnp.where(i == n_q - 1, 2, 1))
    return pl.pallas_call(
        functools.partial(_window_kernel, band=band),
        grid=(nb, n_qblocks // n_t, n_q),
        in_specs=[pl.BlockSpec((1, tq, qw), lambda b, j, i: (b, i, j)),
                  pl.BlockSpec((1, CTX_LEN, LANES), lambda b, j, i: (b, 0, kcol(j))),
                  pl.BlockSpec((1, CTX_LEN, LANES), lambda b, j, i: (b, 0, 2 + kcol(j))),
                  pl.BlockSpec((1, t, LANES), lambda b, j, i: (b, 0, 8 + kcol(j))),
                  pl.BlockSpec((1, t, LANES), lambda b, j, i: (b, 0, 10 + kcol(j))),
                  pl.BlockSpec((1, 2 * tq, band), lambda b, j, i: (case(i), 0, 0)),
                  pl.BlockSpec(sinkv.shape, lambda b, j, i: (0, 0))],
        out_specs=pl.BlockSpec((1, tq, qw), lambda b, j, i: (b, i, j)),
        out_shape=jax.ShapeDtypeStruct((nb, t, n_qblocks * LANES), BF16),
        compiler_params=_params(3),
        name="attn_window",
    )(p_lat, p_ctx, p_ctx, p_lat, p_lat, bias, sinkv)


def _outproj_kernel(*refs, n_in):
    a_refs = refs[:n_in]
    x_ref, mod_ref, gpost_ref, w_ref, o_ref = refs[n_in:]
    y = None
    row = 0
    for a_ref in a_refs:
        k = a_ref.shape[2]
        part = jnp.dot(a_ref[0], w_ref[row:row + k, :], preferred_element_type=F32)
        y = part if y is None else y + part
        row += k
    o_ref[0] = _post_residual(x_ref[0], y, mod_ref, gpost_ref, 1, 1.0)


def _outproj(a_list, x, mod, mod_row, gpost, w, tm):
    nb, t, d = x.shape
    in_specs = [pl.BlockSpec((1, tm, a.shape[2]), lambda b, i: (b, i, 0)) for a in a_list]
    in_specs += [pl.BlockSpec((1, tm, d), lambda b, i: (b, i, 0)),
                 pl.BlockSpec((1, 3 * N_SUB, d), lambda b, i: (mod_row(b), 0, 0)),
                 _const_spec(gpost.shape), _const_spec(w.shape)]
    return pl.pallas_call(
        functools.partial(_outproj_kernel, n_in=len(a_list)),
        grid=(nb, t // tm),
        in_specs=in_specs,
        out_specs=pl.BlockSpec((1, tm, d), lambda b, i: (b, i, 0)),
        out_shape=jax.ShapeDtypeStruct(x.shape, F32),
        compiler_params=_params(2),
        name="out_proj",
    )(*a_list, x, mod, gpost, w)


def _rope_tables(seq):
    rows = seq // GRID_W
    row = np.repeat(np.arange(rows, dtype=np.float64), GRID_W)
    col = np.tile(np.arange(GRID_W, dtype=np.float64), rows)
    inv = ROPE_THETA ** (-np.arange(0, ROPE_HALF, 2, dtype=np.float64) / ROPE_HALF)
    ang = np.concatenate([row[:, None] * inv, col[:, None] * inv], axis=-1)
    cos, sin = np.cos(ang), np.sin(ang)
    cos128 = np.tile(cos, (1, LANES // ROPE_HALF))
    sin128 = np.tile(np.concatenate([-sin, sin], axis=-1), (1, LANES // HEAD_DIM))
    return jnp.asarray(cos128, F32), jnp.asarray(sin128, F32)


def _window_bias(seq, tq, band):
    n_q = seq // tq
    out = np.zeros((3, 2 * tq, band), np.float32)
    for case, i in enumerate((0, 1, n_q - 1)):
        start = min(max(i * tq - WINDOW, 0), seq - band)
        qpos = i * tq + np.arange(tq)[:, None]
        kpos = start + np.arange(band)[None, :]
        m = np.where(np.abs(kpos - qpos) <= WINDOW, 0.0, -np.inf).astype(np.float32)
        out[case] = np.concatenate([m, m], axis=0)
    return jnp.asarray(out)


def _pair_order(n_heads, n_kv):
    g = n_heads // n_kv
    order = []
    for kv in range(0, n_kv, 2):
        for t in range(g):
            order += [kv * g + t, (kv + 1) * g + t]
    return order


def _permute_heads(w, order, axis):
    shape = w.shape
    n = len(order)
    if axis == 1:
        return w.reshape(shape[0], n, HEAD_DIM)[:, np.asarray(order), :].reshape(shape)
    return w.reshape(n, HEAD_DIM, shape[1])[np.asarray(order)].reshape(shape)


def _tile_lanes(v):
    return jnp.tile(v, LANES // v.shape[-1])


def _pad_rows(rows, n=8):
    m = jnp.stack([_tile_lanes(r) if r.shape[-1] != LANES else r for r in rows]).astype(F32)
    return jnp.concatenate([m, jnp.zeros((n - m.shape[0], LANES), F32)], axis=0)


def kernel(x, c, ctx, c_ctx, w_mod, b_mod, g_pre, g_post, w_ffn_gate, w_ffn_up, w_ffn_down,
           w_in_ab, w_out_ab, q_gain_a, k_gain_a, lam_q1, lam_k1, lam_q2, lam_k2, sub_gain_b,
           w_in_c, w_out_c, sink_c):
    nb, seq, d = x.shape
    depth = w_mod.shape[0]
    assert depth == 2, "an odd layer that is not last would also need the windowed mixer's context output"
    n_ctx = ctx.shape[1]
    ctx_row = nb

    cc = jnp.concatenate([c, c_ctx[None, :], jnp.zeros((MOD_ROWS - nb - 1, d), F32)], axis=0)
    mod_all = _modulation(cc, w_mod, b_mod).reshape(depth, MOD_ROWS, 3 * N_SUB, d)

    cos_l, sin_l = _rope_tables(seq)
    scale = HEAD_DIM ** -0.5 * LOG2E

    lat_row = lambda b: b
    ctx_rowf = lambda b: ctx_row
    tm_lat = 512
    tm_ctx = 512
    cos_c = jnp.ones((tm_ctx, LANES), F32)
    sin_c = jnp.zeros((tm_ctx, LANES), F32)
    x_lat = x
    x_ctx = ctx.reshape(1, nb * n_ctx, d)

    for i in range(depth):
        last = i == depth - 1
        mod = mod_all[i]
        gpre, gpost = g_pre[i], g_post[i]
        wg, wu, wd = (w_ffn_gate[i].astype(BF16), w_ffn_up[i].astype(BF16), w_ffn_down[i].astype(BF16))

        x_lat = _ffn(x_lat, mod, lat_row, gpre, gpost, wg[0], wu[0], wd[0], 0, tm_lat)
        x_ctx = _ffn(x_ctx, mod, ctx_rowf, gpre, gpost, wg[0], wu[0], wd[0], 0, tm_ctx)

        if i % 2 == 0:
            e = i // 2
            lam_init = 0.8 - 0.6 * math.exp(-0.3 * i)
            order_a = _pair_order(A_HEADS, A_KV_HEADS)
            n_qa = A_HEADS * HEAD_DIM
            w_in = w_in_ab[e]
            w_in = jnp.concatenate([_permute_heads(w_in[:, :n_qa], order_a, 1), w_in[:, n_qa:]],
                                   axis=1).astype(BF16)
            w_out = w_out_ab[e]
            w_out = jnp.concatenate([_permute_heads(w_out[:n_qa], order_a, 0), w_out[n_qa:]],
                                    axis=0).astype(BF16)
            gains = _pad_rows([q_gain_a[e], k_gain_a[e]])
            lamv = _pad_rows([jnp.pad(v, (0, LANES - HEAD_DIM))
                              for v in (lam_q1[e], lam_k1[e], lam_q2[e], lam_k2[e])])
            subg = _pad_rows([sub_gain_b[e]])
            blocks = ([(0, True, scale)] * 4 + [(1, True, 1.0), (None, False, 1.0)]
                      + [(None, True, scale)] * 4 + [(None, True, 1.0)] * 4 + [(None, False, 1.0)] * 4)
            blocks_ctx = [(g, False, s) for g, _, s in blocks]
            p_lat = _proj(x_lat, mod, lat_row, gpre, w_in, cos_l, sin_l, gains, blocks, tm_lat, True)
            p_ctx = _proj(x_ctx, mod, ctx_rowf, gpre, w_in, cos_c, sin_c, gains, blocks_ctx, tm_ctx,
                          False).reshape(nb, n_ctx, -1)
            qa, qb = (lambda j: j), (lambda j: 6 + j)
            kb, vb = (lambda j: 10 + j), (lambda j: 14 + j)
            a_lat = _attn_ab("gqa", p_lat, [p_ctx, p_lat], qa, 4, 5, 256, n_t=4)
            b_lat = _attn_ab("diff", p_lat, [p_ctx, p_lat], qb, kb, vb, 512, (lamv, subg), lam_init)
            x_lat = _outproj([a_lat, b_lat], x_lat, mod, lat_row, gpost, w_out, tm_lat)
            if not last:
                a_ctx = _attn_ab("gqa", p_ctx, [p_ctx], qa, 4, 5, n_ctx)
                b_ctx = _attn_ab("diff", p_ctx, [p_ctx], qb, kb, vb, n_ctx, (lamv, subg), lam_init)
                x_ctx = _outproj([a_ctx.reshape(1, nb * n_ctx, -1), b_ctx.reshape(1, nb * n_ctx, -1)],
                                 x_ctx, mod, ctx_rowf, gpost, w_out, tm_ctx)
        else:
            o = i // 2
            order_c = _pair_order(C_HEADS, C_KV_HEADS)
            n_qc = C_HEADS * HEAD_DIM
            w_in = w_in_c[o]
            w_in_lat = jnp.concatenate([_permute_heads(w_in[:, :n_qc], order_c, 1), w_in[:, n_qc:]],
                                       axis=1).astype(BF16)
            w_in_ctx = w_in[:, n_qc:].astype(BF16)
            w_out = _permute_heads(w_out_c[o], order_c, 0).astype(BF16)
            sink = sink_c[o][np.asarray(order_c)].astype(F32)
            sinkv = jnp.repeat(sink, LANES).reshape(C_HEADS, LANES)
            gains = jnp.zeros((8, LANES), F32)
            blocks = [(None, True, scale)] * 8 + [(None, True, 1.0)] * 2 + [(None, False, 1.0)] * 2
            blocks_ctx = [(None, False, 1.0)] * 4
            p_lat = _proj(x_lat, mod, lat_row, gpre, w_in_lat, cos_l, sin_l, gains, blocks, tm_lat, True)
            p_ctx = _proj(x_ctx, mod, ctx_rowf, gpre, w_in_ctx, cos_c, sin_c, gains, blocks_ctx, tm_ctx,
                          False).reshape(nb, n_ctx, -1)
            tq = 256
            band = tq + 2 * WINDOW
            c_lat = _attn_window(p_lat, p_ctx, _window_bias(seq, tq, band), sinkv, tq, band, 4)
            x_lat = _outproj([c_lat], x_lat, mod, lat_row, gpost, w_out, tm_lat)

        x_lat = _ffn(x_lat, mod, lat_row, gpre, gpost, wg[1], wu[1], wd[1], 2, tm_lat)
        if not last:
            x_ctx = _ffn(x_ctx, mod, ctx_rowf, gpre, gpost, wg[1], wu[1], wd[1], 2, tm_ctx)
    return x_lat
```

```python
import functools
import math

import numpy as np
import jax
import jax.numpy as jnp
from jax import lax
from jax.experimental import pallas as pl
from jax.experimental.pallas import tpu as pltpu

D_MODEL = 1024
CTX_LEN = 256
GRID_W = 64
HEAD_DIM = 64
ROPE_HALF = HEAD_DIM // 2
ROPE_THETA = 10000.0
A_HEADS = 8
A_KV_HEADS = 2
B_HEADS = 4
C_HEADS = 16
C_KV_HEADS = 4
WINDOW = 128
D_FF = 2816
N_SUB = 3
HALF_STEP = 0.5
EPS = 1e-6

LANES = 128
MOD_ROWS = 24
VMEM_LIMIT = 56 * 1024 * 1024
FFN_ROWS = 256
KEY_CHUNK = 1024
LOG2E = 1.4426950408889634

F32 = jnp.float32
BF16 = jnp.bfloat16
NT_DIMS = (((1,), (1,)), ((), ()))


def _params(n_grid):
    return pltpu.CompilerParams(dimension_semantics=("parallel",) * n_grid,
                                vmem_limit_bytes=VMEM_LIMIT)


def _rms_rows(x):
    return x * lax.rsqrt(jnp.mean(x * x, axis=-1, keepdims=True) + EPS)


def _pre_norm(x, mod_ref, gpre_ref, sub):
    shift = mod_ref[0, 3 * sub + 0:3 * sub + 1, :]
    scale = mod_ref[0, 3 * sub + 1:3 * sub + 2, :]
    return _rms_rows(x) * gpre_ref[sub:sub + 1, :] * (1.0 + scale) + shift


def _post_residual(x, y, mod_ref, gpost_ref, sub, res_w):
    gate = mod_ref[0, 3 * sub + 2:3 * sub + 3, :]
    return x + res_w * gate * (_rms_rows(y) * gpost_ref[sub:sub + 1, :])


def _mod_kernel(c_ref, w_ref, b_ref, o_ref):
    c = c_ref[...]
    act = (c * jax.nn.sigmoid(c)).astype(BF16)
    o_ref[0] = jnp.dot(act, w_ref[0].astype(BF16), preferred_element_type=F32) + b_ref[0]


def _modulation(cc, w_mod, b_mod):
    depth, d, n = w_mod.shape
    tn = 1152
    return pl.pallas_call(
        _mod_kernel,
        grid=(depth, n // tn),
        in_specs=[pl.BlockSpec((MOD_ROWS, d), lambda i, j: (0, 0)),
                  pl.BlockSpec((1, d, tn), lambda i, j: (i, 0, j)),
                  pl.BlockSpec((1, 1, tn), lambda i, j: (i, 0, j))],
        out_specs=pl.BlockSpec((1, MOD_ROWS, tn), lambda i, j: (i, 0, j)),
        out_shape=jax.ShapeDtypeStruct((depth, MOD_ROWS, n), F32),
        compiler_params=_params(2),
        name="adaln_mod",
    )(cc, w_mod, b_mod.reshape(depth, 1, n))


def _mixer_residual(x, a_refs, r, rows, w_ref, mod_ref, gpost_ref):
    y = None
    row = 0
    for a_ref in a_refs:
        k = a_ref.shape[2]
        part = jnp.dot(a_ref[0, r:r + rows, :], w_ref[row:row + k, :], preferred_element_type=F32)
        y = part if y is None else y + part
        row += k
    return _post_residual(x, y, mod_ref, gpost_ref, 1, 1.0)


def _ffn_kernel(*refs, sub, rows, n_mix):
    a_refs = refs[:n_mix]
    x_ref, mod_ref, gpre_ref, gpost_ref = refs[n_mix:n_mix + 4]
    wout_ref = refs[n_mix + 4] if n_mix else None
    wg_ref, wu_ref, wd_ref, o_ref = refs[-4:]
    for r in range(0, x_ref.shape[1], rows):
        x = x_ref[0, r:r + rows, :]
        if n_mix:
            x = _mixer_residual(x, a_refs, r, rows, wout_ref, mod_ref, gpost_ref)
        hb = _pre_norm(x, mod_ref, gpre_ref, sub).astype(BF16)
        g = jnp.dot(hb, wg_ref[...], preferred_element_type=F32)
        u = jnp.dot(hb, wu_ref[...], preferred_element_type=F32)
        a = (g * jax.nn.sigmoid(g) * u).astype(BF16)
        y = jnp.dot(a, wd_ref[...], preferred_element_type=F32)
        o_ref[0, r:r + rows, :] = _post_residual(x, y, mod_ref, gpost_ref, sub, HALF_STEP)


def _const_spec(shape):
    return pl.BlockSpec(shape, lambda *_: (0,) * len(shape), pipeline_mode=pl.Buffered(1))


def _slab_spec(arr, lead):
    n = len(lead)
    return pl.BlockSpec((None,) * n + arr.shape[n:], lambda *_: tuple(lead) + (0,) * (arr.ndim - n),
                        pipeline_mode=pl.Buffered(1))


def _ffn(x, mod, mod_row, gpre, gpost, wg, wu, wd, widx, sub, tm, mix=(), w_out=None):
    nb, t, d = x.shape
    kern = functools.partial(_ffn_kernel, sub=sub, rows=FFN_ROWS, n_mix=len(mix))
    in_specs = [pl.BlockSpec((1, tm, a.shape[2]), lambda b, i: (b, i, 0)) for a in mix]
    in_specs += [pl.BlockSpec((1, tm, d), lambda b, i: (b, i, 0)),
                 pl.BlockSpec((1, 3 * N_SUB, d), lambda b, i: (mod_row(b), 0, 0)),
                 _const_spec(gpre.shape), _const_spec(gpost.shape)]
    in_specs += [_const_spec(w_out.shape)] if mix else []
    in_specs += [_slab_spec(wg, widx), _slab_spec(wu, widx), _slab_spec(wd, widx)]
    return pl.pallas_call(
        kern,
        grid=(nb, t // tm),
        in_specs=in_specs,
        out_specs=pl.BlockSpec((1, tm, d), lambda b, i: (b, i, 0)),
        out_shape=jax.ShapeDtypeStruct(x.shape, F32),
        compiler_params=_params(2),
        name="swiglu_sublayer",
    )(*mix, x, mod, gpre, gpost, *([w_out] if mix else []), wg, wu, wd)


def _proj_kernel(x_ref, mod_ref, gpre_ref, w_ref, cos_ref, sin_ref, gain_ref, o_ref, *, blocks):
    x = x_ref[0]
    tm = x.shape[0]
    hb = _pre_norm(x, mod_ref, gpre_ref, 1).astype(BF16)
    r = jnp.dot(hb, w_ref[...], preferred_element_type=F32)
    lane = lax.broadcasted_iota(jnp.int32, (tm, LANES), 1)
    lo = lane < HEAD_DIM
    first_half = (lane & (HEAD_DIM - 1)) < ROPE_HALF
    for j, (gain_row, rotary, scale) in enumerate(blocks):
        v = r[:, LANES * j:LANES * (j + 1)]
        if gain_row is not None:
            v2 = v * v
            ms_lo = jnp.sum(jnp.where(lo, v2, 0.0), axis=-1, keepdims=True)
            ms_hi = jnp.sum(jnp.where(lo, 0.0, v2), axis=-1, keepdims=True)
            ms = jnp.where(lo, ms_lo, ms_hi) * (1.0 / HEAD_DIM)
            v = v * lax.rsqrt(ms + EPS) * gain_ref[gain_row:gain_row + 1, :]
        if rotary:
            partner = jnp.where(first_half, pltpu.roll(v, LANES - ROPE_HALF, 1),
                                pltpu.roll(v, ROPE_HALF, 1))
            v = v * cos_ref[...] + partner * sin_ref[...]
        if scale != 1.0:
            v = v * scale
        o_ref[0, :, LANES * j:LANES * (j + 1)] = v.astype(BF16)


def _proj(x, mod, mod_row, gpre, w, cos, sin, gains, blocks, tm, rope_rows):
    nb, t, d = x.shape
    n = w.shape[1]
    kern = functools.partial(_proj_kernel, blocks=tuple(blocks))
    rope_map = (lambda b, i: (i, 0)) if rope_rows else (lambda b, i: (0, 0))
    return pl.pallas_call(
        kern,
        grid=(nb, t // tm),
        in_specs=[pl.BlockSpec((1, tm, d), lambda b, i: (b, i, 0)),
                  pl.BlockSpec((1, 3 * N_SUB, d), lambda b, i: (mod_row(b), 0, 0)),
                  _const_spec(gpre.shape), _const_spec(w.shape),
                  pl.BlockSpec((tm, LANES), rope_map),
                  pl.BlockSpec((tm, LANES), rope_map),
                  _const_spec(gains.shape)],
        out_specs=pl.BlockSpec((1, tm, n), lambda b, i: (b, i, 0)),
        out_shape=jax.ShapeDtypeStruct((nb, t, n), BF16),
        compiler_params=_params(2),
        name="qkv_proj",
    )(x, mod, gpre, w, cos, sin, gains)


def _split_heads(q):
    lane = lax.broadcasted_iota(jnp.int32, q.shape, 1)
    zero = jnp.zeros_like(q)
    return jnp.concatenate([jnp.where(lane < HEAD_DIM, q, zero),
                            jnp.where(lane < HEAD_DIM, zero, q)], axis=0)


def _kv_chunks(kv_refs, n_kv, cols=slice(0, LANES)):
    chunks = []
    for k_ref, v_ref in zip(kv_refs[:n_kv], kv_refs[n_kv:]):
        n = k_ref.shape[1]
        step = min(n, KEY_CHUNK)
        chunks += [(k_ref, v_ref, pl.ds(a, step), cols, None) for a in range(0, n, step)]
    return chunks


def _streaming_attention(problems):
    items = [(pi, ci) for pi, (_, chunks) in enumerate(problems) for ci in range(len(chunks))]
    state = [[None, None] for _ in problems]
    scores, probs = {}, {}

    def stage_scores(w):
        pi, ci = items[w]
        qq, chunks = problems[pi]
        k_ref, _, rows, cols, bias_ref = chunks[ci]
        s = lax.dot_general(qq, k_ref[0, rows, cols], NT_DIMS, preferred_element_type=F32)
        scores[w] = s if bias_ref is None else s + bias_ref[0]

    def stage_exp(w):
        pi, _ = items[w]
        m = state[pi][0]
        s = scores.pop(w)
        s_max = jnp.max(s, axis=-1, keepdims=True)
        m_new = s_max if m is None else jnp.maximum(m, s_max)
        alpha = None if m is None else jnp.exp2(m - m_new)
        state[pi][0] = m_new
        probs[w] = (jnp.exp2(s - m_new).astype(BF16), alpha)

    def stage_values(w):
        pi, ci = items[w]
        _, v_ref, rows, cols, _ = problems[pi][1][ci]
        p, alpha = probs.pop(w)
        v = v_ref[0, rows, cols]
        pv = jnp.dot(p, jnp.concatenate([v, jnp.ones_like(v)], axis=1), preferred_element_type=F32)
        state[pi][1] = pv if alpha is None else alpha * state[pi][1] + pv

    n = len(items)
    for step in range(n + 2):
        if step < n:
            stage_scores(step)
        if 0 <= step - 1 < n:
            stage_exp(step - 1)
        if 0 <= step - 2 < n:
            stage_values(step - 2)
    return [(acc[:, :LANES], acc[:, LANES:], m) for m, acc in state]


def _merge_heads(o2, tq):
    lane = lax.broadcasted_iota(jnp.int32, (tq, LANES), 1)
    return jnp.where(lane < HEAD_DIM, o2[:tq], o2[tq:])


def _gqa_kernel(*refs, n_kv):
    q_ref, kv_refs, o_ref = refs[0], refs[1:1 + 2 * n_kv], refs[-1]
    tq = q_ref.shape[1]
    chunks = _kv_chunks(kv_refs, n_kv)
    cols = [slice(LANES * t, LANES * (t + 1)) for t in range(q_ref.shape[2] // LANES)]
    results = _streaming_attention([(_split_heads(q_ref[0, :, c]), chunks) for c in cols])
    for c, (acc, l, _) in zip(cols, results):
        o_ref[0, :, c] = _merge_heads(acc / l, tq).astype(BF16)


def _diff_kernel(*refs, n_kv, lam_init):
    q_ref, kv_refs = refs[0], refs[1:1 + 2 * n_kv]
    lam_ref, gain_ref, o_ref = refs[-3], refs[-2], refs[-1]
    tq = q_ref.shape[1]
    lv = lam_ref[...]
    lam = (jnp.exp(jnp.sum(lv[0:1] * lv[1:2], axis=-1, keepdims=True))
           - jnp.exp(jnp.sum(lv[2:3] * lv[3:4], axis=-1, keepdims=True)) + lam_init)
    cols = [slice(LANES * t, LANES * (t + 1)) for t in range(q_ref.shape[2] // LANES)]
    results = _streaming_attention(
        [(_split_heads(q_ref[0, :, c]), _kv_chunks(kv_refs, n_kv, c)) for c in cols])
    for c, (acc, l, _) in zip(cols, results):
        o2 = acc / l
        o = o2[:tq] - lam * o2[tq:]
        o = _rms_rows(o) * gain_ref[0:1, :] * (1.0 - lam_init)
        o_ref[0, :, c] = o.astype(BF16)


def _window_kernel(q_ref, kc_ref, vc_ref, kl_ref, vl_ref, bias_ref, sink_ref, o_ref, *, band):
    tq = q_ref.shape[1]
    seq = kl_ref.shape[1]
    n_t = q_ref.shape[2] // LANES
    i = pl.program_id(2)
    start = pl.multiple_of(jnp.clip(i * tq - WINDOW, 0, seq - band), LANES)
    cols = slice(0, LANES)
    chunks = [(kc_ref, vc_ref, pl.ds(0, kc_ref.shape[1]), cols, None),
              (kl_ref, vl_ref, pl.ds(start, band), cols, bias_ref)]
    problems = [(_split_heads(q_ref[0, :, LANES * t:LANES * (t + 1)]), chunks) for t in range(n_t)]
    lane = lax.broadcasted_iota(jnp.int32, (tq, LANES), 1)
    for t, (acc, l, m) in enumerate(_streaming_attention(problems)):
        head = 2 * (pl.program_id(1) * n_t + t)
        l_lo = l[:tq] + jnp.exp2(sink_ref[pl.ds(head, 1), :] * LOG2E - m[:tq])
        l_hi = l[tq:] + jnp.exp2(sink_ref[pl.ds(head + 1, 1), :] * LOG2E - m[tq:])
        o = jnp.where(lane < HEAD_DIM, acc[:tq] / l_lo, acc[tq:] / l_hi)
        o_ref[0, :, LANES * t:LANES * (t + 1)] = o.astype(BF16)


def _col_spec(rows, col, width):
    if callable(col):
        return pl.BlockSpec((1, rows, width), lambda b, j, i: (b, 0, col(j)))
    return pl.BlockSpec((1, rows, width), lambda b, j, i: (b, 0, col))


def _attn_ab(kind, q_arr, kv_arrs, q_col, k_col, v_col, tq, extra=(), lam_init=0.0, n_t=1):
    nb, t, _ = q_arr.shape
    n_kv = len(kv_arrs)
    qw = n_t * LANES
    kvw = LANES if kind == "gqa" else qw
    n_blocks = 4 // n_t
    in_specs = [pl.BlockSpec((1, tq, qw), lambda b, j, i: (b, i, q_col(j)))]
    in_specs += [_col_spec(a.shape[1], k_col, kvw) for a in kv_arrs]
    in_specs += [_col_spec(a.shape[1], v_col, kvw) for a in kv_arrs]
    in_specs += [pl.BlockSpec(e.shape, lambda b, j, i: (0, 0)) for e in extra]
    if kind == "gqa":
        kern = functools.partial(_gqa_kernel, n_kv=n_kv)
    else:
        kern = functools.partial(_diff_kernel, n_kv=n_kv, lam_init=lam_init)
    return pl.pallas_call(
        kern,
        grid=(nb, n_blocks, t // tq),
        in_specs=in_specs,
        out_specs=pl.BlockSpec((1, tq, qw), lambda b, j, i: (b, i, j)),
        out_shape=jax.ShapeDtypeStruct((nb, t, n_blocks * qw), BF16),
        compiler_params=_params(3),
        name="attn_" + kind,
    )(q_arr, *kv_arrs, *kv_arrs, *extra)


def _attn_window(p_lat, p_ctx, bias, sinkv, tq, band, n_t):
    nb, t, _ = p_lat.shape
    n_q = t // tq
    n_qblocks = sinkv.shape[0] // 2
    qw = n_t * LANES
    kcol = lambda j: (j * n_t) // 4
    case = lambda i: jnp.where(i == 0, 0, jnp.where(i == n_q - 1, 2, 1))
    return pl.pallas_call(
        functools.partial(_window_kernel, band=band),
        grid=(nb, n_qblocks // n_t, n_q),
        in_specs=[pl.BlockSpec((1, tq, qw), lambda b, j, i: (b, i, j)),
                  pl.BlockSpec((1, CTX_LEN, LANES), lambda b, j, i: (b, 0, kcol(j))),
                  pl.BlockSpec((1, CTX_LEN, LANES), lambda b, j, i: (b, 0, 2 + kcol(j))),
                  pl.BlockSpec((1, t, LANES), lambda b, j, i: (b, 0, 8 + kcol(j))),
                  pl.BlockSpec((1, t, LANES), lambda b, j, i: (b, 0, 10 + kcol(j))),
                  pl.BlockSpec((1, 2 * tq, band), lambda b, j, i: (case(i), 0, 0)),
                  pl.BlockSpec(sinkv.shape, lambda b, j, i: (0, 0))],
        out_specs=pl.BlockSpec((1, tq, qw), lambda b, j, i: (b, i, j)),
        out_shape=jax.ShapeDtypeStruct((nb, t, n_qblocks * LANES), BF16),
        compiler_params=_params(3),
        name="attn_window",
    )(p_lat, p_ctx, p_ctx, p_lat, p_lat, bias, sinkv)


def _rope_tables(seq):
    rows = seq // GRID_W
    row = np.repeat(np.arange(rows, dtype=np.float64), GRID_W)
    col = np.tile(np.arange(GRID_W, dtype=np.float64), rows)
    inv = ROPE_THETA ** (-np.arange(0, ROPE_HALF, 2, dtype=np.float64) / ROPE_HALF)
    ang = np.concatenate([row[:, None] * inv, col[:, None] * inv], axis=-1)
    cos, sin = np.cos(ang), np.sin(ang)
    cos128 = np.tile(cos, (1, LANES // ROPE_HALF))
    sin128 = np.tile(np.concatenate([-sin, sin], axis=-1), (1, LANES // HEAD_DIM))
    return jnp.asarray(cos128, F32), jnp.asarray(sin128, F32)


def _window_bias(seq, tq, band):
    n_q = seq // tq
    out = np.zeros((3, 2 * tq, band), np.float32)
    for case, i in enumerate((0, 1, n_q - 1)):
        start = min(max(i * tq - WINDOW, 0), seq - band)
        qpos = i * tq + np.arange(tq)[:, None]
        kpos = start + np.arange(band)[None, :]
        m = np.where(np.abs(kpos - qpos) <= WINDOW, 0.0, -np.inf).astype(np.float32)
        out[case] = np.concatenate([m, m], axis=0)
    return jnp.asarray(out)


def _pair_order(n_heads, n_kv):
    g = n_heads // n_kv
    order = []
    for kv in range(0, n_kv, 2):
        for t in range(g):
            order += [kv * g + t, (kv + 1) * g + t]
    return order


def _permute_heads(w, order, axis):
    shape = w.shape
    n = len(order)
    if axis == 1:
        return w.reshape(shape[0], n, HEAD_DIM)[:, np.asarray(order), :].reshape(shape)
    return w.reshape(n, HEAD_DIM, shape[1])[np.asarray(order)].reshape(shape)


def _tile_lanes(v):
    return jnp.tile(v, LANES // v.shape[-1])


def _pad_rows(rows, n=8):
    m = jnp.stack([_tile_lanes(r) if r.shape[-1] != LANES else r for r in rows]).astype(F32)
    return jnp.concatenate([m, jnp.zeros((n - m.shape[0], LANES), F32)], axis=0)


def kernel(x, c, ctx, c_ctx, w_mod, b_mod, g_pre, g_post, w_ffn_gate, w_ffn_up, w_ffn_down,
           w_in_ab, w_out_ab, q_gain_a, k_gain_a, lam_q1, lam_k1, lam_q2, lam_k2, sub_gain_b,
           w_in_c, w_out_c, sink_c):
    nb, seq, d = x.shape
    depth = w_mod.shape[0]
    assert depth == 2, "an odd layer that is not last would also need the windowed mixer's context output"
    n_ctx = ctx.shape[1]
    ctx_row = nb

    cc = jnp.concatenate([c, c_ctx[None, :], jnp.zeros((MOD_ROWS - nb - 1, d), F32)], axis=0)
    mod_all = _modulation(cc, w_mod, b_mod).reshape(depth, MOD_ROWS, 3 * N_SUB, d)

    cos_l, sin_l = _rope_tables(seq)
    scale = HEAD_DIM ** -0.5 * LOG2E

    lat_row = lambda b: b
    ctx_rowf = lambda b: ctx_row
    tm_lat = 512
    tm_ctx = 512
    tm_ffn = 1024
    cos_c = jnp.ones((tm_ctx, LANES), F32)
    sin_c = jnp.zeros((tm_ctx, LANES), F32)
    x_lat = x
    x_ctx = ctx.reshape(1, nb * n_ctx, d)
    wg, wu, wd = w_ffn_gate.astype(BF16), w_ffn_up.astype(BF16), w_ffn_down.astype(BF16)

    for i in range(depth):
        last = i == depth - 1
        mod = mod_all[i]
        gpre, gpost = g_pre[i], g_post[i]

        x_lat = _ffn(x_lat, mod, lat_row, gpre, gpost, wg, wu, wd, (i, 0), 0, tm_ffn)
        x_ctx = _ffn(x_ctx, mod, ctx_rowf, gpre, gpost, wg, wu, wd, (i, 0), 0, tm_ffn)

        if i % 2 == 0:
            e = i // 2
            lam_init = 0.8 - 0.6 * math.exp(-0.3 * i)
            order_a = _pair_order(A_HEADS, A_KV_HEADS)
            n_qa = A_HEADS * HEAD_DIM
            w_in = w_in_ab[e]
            w_in = jnp.concatenate([_permute_heads(w_in[:, :n_qa], order_a, 1), w_in[:, n_qa:]],
                                   axis=1).astype(BF16)
            w_out = w_out_ab[e]
            w_out = jnp.concatenate([_permute_heads(w_out[:n_qa], order_a, 0), w_out[n_qa:]],
                                    axis=0).astype(BF16)
            gains = _pad_rows([q_gain_a[e], k_gain_a[e]])
            lamv = _pad_rows([jnp.pad(v, (0, LANES - HEAD_DIM))
                              for v in (lam_q1[e], lam_k1[e], lam_q2[e], lam_k2[e])])
            subg = _pad_rows([sub_gain_b[e]])
            blocks = ([(0, True, scale)] * 4 + [(1, True, 1.0), (None, False, 1.0)]
                      + [(None, True, scale)] * 4 + [(None, True, 1.0)] * 4 + [(None, False, 1.0)] * 4)
            blocks_ctx = [(g, False, s) for g, _, s in blocks]
            p_lat = _proj(x_lat, mod, lat_row, gpre, w_in, cos_l, sin_l, gains, blocks, tm_lat, True)
            p_ctx = _proj(x_ctx, mod, ctx_rowf, gpre, w_in, cos_c, sin_c, gains, blocks_ctx, tm_ctx,
                          False).reshape(nb, n_ctx, -1)
            qa = lambda j: j
            qb, kb, vb = (lambda j: 3 + j), (lambda j: 5 + j), (lambda j: 7 + j)
            a_lat = _attn_ab("gqa", p_lat, [p_ctx, p_lat], qa, 4, 5, 256, n_t=4)
            b_lat = _attn_ab("diff", p_lat, [p_ctx, p_lat], qb, kb, vb, 512, (lamv, subg), lam_init, n_t=2)
            mix_lat = [a_lat, b_lat]
            if not last:
                a_ctx = _attn_ab("gqa", p_ctx, [p_ctx], qa, 4, 5, n_ctx, n_t=4)
                b_ctx = _attn_ab("diff", p_ctx, [p_ctx], qb, kb, vb, n_ctx, (lamv, subg), lam_init, n_t=2)
                mix_ctx = [a_ctx.reshape(1, nb * n_ctx, -1), b_ctx.reshape(1, nb * n_ctx, -1)]
        else:
            o = i // 2
            order_c = _pair_order(C_HEADS, C_KV_HEADS)
            n_qc = C_HEADS * HEAD_DIM
            w_in = w_in_c[o]
            w_in_lat = jnp.concatenate([_permute_heads(w_in[:, :n_qc], order_c, 1), w_in[:, n_qc:]],
                                       axis=1).astype(BF16)
            w_in_ctx = w_in[:, n_qc:].astype(BF16)
            w_out = _permute_heads(w_out_c[o], order_c, 0).astype(BF16)
            sink = sink_c[o][np.asarray(order_c)].astype(F32)
            sinkv = jnp.repeat(sink, LANES).reshape(C_HEADS, LANES)
            gains = jnp.zeros((8, LANES), F32)
            blocks = [(None, True, scale)] * 8 + [(None, True, 1.0)] * 2 + [(None, False, 1.0)] * 2
            blocks_ctx = [(None, False, 1.0)] * 4
            p_lat = _proj(x_lat, mod, lat_row, gpre, w_in_lat, cos_l, sin_l, gains, blocks, tm_lat, True)
            p_ctx = _proj(x_ctx, mod, ctx_rowf, gpre, w_in_ctx, cos_c, sin_c, gains, blocks_ctx, tm_ctx,
                          False).reshape(nb, n_ctx, -1)
            tq = 256
            band = tq + 2 * WINDOW
            c_lat = _attn_window(p_lat, p_ctx, _window_bias(seq, tq, band), sinkv, tq, band, 4)
            mix_lat = [c_lat]

        x_lat = _ffn(x_lat, mod, lat_row, gpre, gpost, wg, wu, wd, (i, 1), 2, tm_ffn, mix_lat, w_out)
        if not last:
            x_ctx = _ffn(x_ctx, mod, ctx_rowf, gpre, gpost, wg, wu, wd, (i, 1), 2, tm_ffn, mix_ctx, w_out)
    return x_lat
```

```python
import functools
import math

import numpy as np
import jax
import jax.numpy as jnp
from jax import lax
from jax.experimental import pallas as pl
from jax.experimental.pallas import tpu as pltpu

D_MODEL = 1024
CTX_LEN = 256
GRID_W = 64
HEAD_DIM = 64
ROPE_HALF = HEAD_DIM // 2
ROPE_THETA = 10000.0
A_HEADS = 8
A_KV_HEADS = 2
B_HEADS = 4
C_HEADS = 16
C_KV_HEADS = 4
WINDOW = 128
D_FF = 2816
N_SUB = 3
HALF_STEP = 0.5
EPS = 1e-6

LANES = 128
MOD_ROWS = 24
VMEM_LIMIT = 56 * 1024 * 1024
FFN_ROWS = 256
KEY_CHUNK = 1024
Q_PER_KV_BLOCK = 4
LOG2E = 1.4426950408889634

F32 = jnp.float32
BF16 = jnp.bfloat16
NT_DIMS = (((1,), (1,)), ((), ()))


def _params(n_grid):
    return pltpu.CompilerParams(dimension_semantics=("parallel",) * n_grid,
                                vmem_limit_bytes=VMEM_LIMIT)


def _rms_rows(x):
    return x * lax.rsqrt(jnp.mean(x * x, axis=-1, keepdims=True) + EPS)


def _pre_norm(x, mod_ref, gpre_ref, sub):
    shift = mod_ref[0, 3 * sub + 0:3 * sub + 1, :]
    scale = mod_ref[0, 3 * sub + 1:3 * sub + 2, :]
    return _rms_rows(x) * gpre_ref[sub:sub + 1, :] * (1.0 + scale) + shift


def _post_residual(x, y, mod_ref, gpost_ref, sub, res_w):
    gate = mod_ref[0, 3 * sub + 2:3 * sub + 3, :]
    return x + res_w * gate * (_rms_rows(y) * gpost_ref[sub:sub + 1, :])


def _mod_kernel(c_ref, w_ref, b_ref, o_ref):
    c = c_ref[...]
    act = (c * jax.nn.sigmoid(c)).astype(BF16)
    o_ref[0] = jnp.dot(act, w_ref[0].astype(BF16), preferred_element_type=F32) + b_ref[0]


def _modulation(cc, w_mod, b_mod):
    depth, d, n = w_mod.shape
    tn = 1152
    return pl.pallas_call(
        _mod_kernel,
        grid=(depth, n // tn),
        in_specs=[pl.BlockSpec((MOD_ROWS, d), lambda i, j: (0, 0)),
                  pl.BlockSpec((1, d, tn), lambda i, j: (i, 0, j)),
                  pl.BlockSpec((1, 1, tn), lambda i, j: (i, 0, j))],
        out_specs=pl.BlockSpec((1, MOD_ROWS, tn), lambda i, j: (i, 0, j)),
        out_shape=jax.ShapeDtypeStruct((depth, MOD_ROWS, n), F32),
        compiler_params=_params(2),
        name="adaln_mod",
    )(cc, w_mod, b_mod.reshape(depth, 1, n))


def _mixer_residual(x, a_refs, r, rows, w_ref, mod_ref, gpost_ref):
    y = None
    row = 0
    for a_ref in a_refs:
        k = a_ref.shape[2]
        part = jnp.dot(a_ref[0, r:r + rows, :], w_ref[row:row + k, :], preferred_element_type=F32)
        y = part if y is None else y + part
        row += k
    return _post_residual(x, y, mod_ref, gpost_ref, 1, 1.0)


def _ffn_kernel(*refs, sub, rows, n_mix):
    a_refs = refs[:n_mix]
    x_ref, mod_ref, gpre_ref, gpost_ref = refs[n_mix:n_mix + 4]
    wout_ref = refs[n_mix + 4] if n_mix else None
    wg_ref, wu_ref, wd_ref, o_ref = refs[-4:]
    starts = range(0, x_ref.shape[1], rows)
    xs = [x_ref[0, r:r + rows, :] for r in starts]
    if n_mix:
        xs = [_mixer_residual(x, a_refs, r, rows, wout_ref, mod_ref, gpost_ref) for x, r in zip(xs, starts)]
    for x, r in zip(xs, starts):
        hb = _pre_norm(x, mod_ref, gpre_ref, sub).astype(BF16)
        g = jnp.dot(hb, wg_ref[...], preferred_element_type=F32)
        u = jnp.dot(hb, wu_ref[...], preferred_element_type=F32)
        a = (g * jax.nn.sigmoid(g) * u).astype(BF16)
        y = jnp.dot(a, wd_ref[...], preferred_element_type=F32)
        o_ref[0, r:r + rows, :] = _post_residual(x, y, mod_ref, gpost_ref, sub, HALF_STEP)


def _const_spec(shape):
    return pl.BlockSpec(shape, lambda *_: (0,) * len(shape), pipeline_mode=pl.Buffered(1))


def _slab_spec(arr, lead):
    n = len(lead)
    return pl.BlockSpec((None,) * n + arr.shape[n:], lambda *_: tuple(lead) + (0,) * (arr.ndim - n),
                        pipeline_mode=pl.Buffered(1))


def _ffn(x, mod, mod_row, gpre, gpost, wg, wu, wd, widx, sub, tm, mix=(), w_out=None):
    nb, t, d = x.shape
    kern = functools.partial(_ffn_kernel, sub=sub, rows=FFN_ROWS, n_mix=len(mix))
    row_spec = lambda width: pl.BlockSpec((1, tm, width), lambda b, i: (b, i, 0))
    in_specs = [row_spec(a.shape[2]) for a in mix]
    in_specs += [row_spec(d),
                 pl.BlockSpec((1, 3 * N_SUB, d), lambda b, i: (mod_row(b), 0, 0)),
                 _const_spec(gpre.shape), _const_spec(gpost.shape)]
    in_specs += [_const_spec(w_out.shape)] if mix else []
    in_specs += [_slab_spec(wg, widx), _slab_spec(wu, widx), _slab_spec(wd, widx)]
    return pl.pallas_call(
        kern,
        grid=(nb, t // tm),
        in_specs=in_specs,
        out_specs=row_spec(d),
        out_shape=jax.ShapeDtypeStruct(x.shape, F32),
        compiler_params=_params(2),
        name="swiglu_sublayer",
    )(*mix, x, mod, gpre, gpost, *([w_out] if mix else []), wg, wu, wd)


def _proj_kernel(x_ref, mod_ref, gpre_ref, w_ref, cos_ref, sin_ref, gain_ref, o_ref, *, blocks):
    x = x_ref[0]
    rows = x.shape[0]
    hb = _pre_norm(x, mod_ref, gpre_ref, 1).astype(BF16)
    res = jnp.dot(hb, w_ref[...], preferred_element_type=F32)
    lane = lax.broadcasted_iota(jnp.int32, (rows, LANES), 1)
    lo = _first_head(lane)
    for j, (gain_row, rotary, scale) in enumerate(blocks):
        v = res[:, LANES * j:LANES * (j + 1)]
        if gain_row is not None:
            v2 = v * v
            ms_lo = jnp.sum(jnp.where(lo, v2, 0.0), axis=-1, keepdims=True)
            ms_hi = jnp.sum(jnp.where(lo, 0.0, v2), axis=-1, keepdims=True)
            ms = jnp.where(lo, ms_lo, ms_hi) * (1.0 / HEAD_DIM)
            v = v * lax.rsqrt(ms + EPS) * gain_ref[gain_row:gain_row + 1, :]
        if rotary:
            v = v * cos_ref[...] + pltpu.roll(v, LANES // 2, 1) * sin_ref[...]
        if scale != 1.0:
            v = v * scale
        o_ref[0, :, LANES * j:LANES * (j + 1)] = v.astype(BF16)


def _proj(x, mod, mod_row, gpre, w, cos, sin, gains, blocks, tm, rope_rows):
    nb, t, d = x.shape
    n = w.shape[1]
    kern = functools.partial(_proj_kernel, blocks=tuple(blocks))
    rope_map = (lambda b, i: (i, 0)) if rope_rows else (lambda b, i: (0, 0))
    return pl.pallas_call(
        kern,
        grid=(nb, t // tm),
        in_specs=[pl.BlockSpec((1, tm, d), lambda b, i: (b, i, 0)),
                  pl.BlockSpec((1, 3 * N_SUB, d), lambda b, i: (mod_row(b), 0, 0)),
                  _const_spec(gpre.shape), _const_spec(w.shape),
                  pl.BlockSpec((tm, LANES), rope_map),
                  pl.BlockSpec((tm, LANES), rope_map),
                  _const_spec(gains.shape)],
        out_specs=pl.BlockSpec((1, tm, n), lambda b, i: (b, i, 0)),
        out_shape=jax.ShapeDtypeStruct((nb, t, n), BF16),
        compiler_params=_params(2),
        name="qkv_proj",
    )(x, mod, gpre, w, cos, sin, gains)


def _first_head(lane):
    return (lane & ROPE_HALF) == 0


def _split_heads(q):
    first = _first_head(lax.broadcasted_iota(jnp.int32, q.shape, 1))
    zero = jnp.zeros_like(q)
    return jnp.concatenate([jnp.where(first, q, zero), jnp.where(first, zero, q)], axis=0)


def _kv_chunks(kv_refs, n_kv, cols=slice(0, LANES)):
    chunks = []
    for k_ref, v_ref in zip(kv_refs[:n_kv], kv_refs[n_kv:]):
        n = k_ref.shape[1]
        step = min(n, KEY_CHUNK)
        chunks += [(k_ref, v_ref, pl.ds(a, step), cols, None) for a in range(0, n, step)]
    return chunks


def _streaming_attention(problems):
    items = [(pi, ci) for pi, (_, chunks) in enumerate(problems) for ci in range(len(chunks))]
    state = [[None, None] for _ in problems]
    scores, probs = {}, {}

    def stage_scores(w):
        pi, ci = items[w]
        qq, chunks = problems[pi]
        k_ref, _, rows, cols, bias_ref = chunks[ci]
        s = lax.dot_general(qq, k_ref[0, rows, cols], NT_DIMS, preferred_element_type=F32)
        scores[w] = s if bias_ref is None else s + bias_ref[0]

    def stage_exp(w):
        pi, _ = items[w]
        m = state[pi][0]
        s = scores.pop(w)
        s_max = jnp.max(s, axis=-1, keepdims=True)
        m_new = s_max if m is None else jnp.maximum(m, s_max)
        alpha = None if m is None else jnp.exp2(m - m_new)
        state[pi][0] = m_new
        probs[w] = (jnp.exp2(s - m_new).astype(BF16), alpha)

    def stage_values(w):
        pi, ci = items[w]
        _, v_ref, rows, cols, _ = problems[pi][1][ci]
        p, alpha = probs.pop(w)
        v = v_ref[0, rows, cols]
        pv = jnp.dot(p, jnp.concatenate([v, jnp.ones_like(v)], axis=1), preferred_element_type=F32)
        state[pi][1] = pv if alpha is None else alpha * state[pi][1] + pv

    n = len(items)
    for step in range(n + 2):
        if step < n:
            stage_scores(step)
        if 0 <= step - 1 < n:
            stage_exp(step - 1)
        if 0 <= step - 2 < n:
            stage_values(step - 2)
    return [(acc[:, :LANES], acc[:, LANES:], m) for m, acc in state]


def _merge_heads(o2, tq):
    lane = lax.broadcasted_iota(jnp.int32, (tq, LANES), 1)
    return jnp.where(lane < HEAD_DIM, o2[:tq], o2[tq:])


def _gqa_kernel(*refs, n_kv):
    q_ref, kv_refs, o_ref = refs[0], refs[1:1 + 2 * n_kv], refs[-1]
    tq = q_ref.shape[1]
    chunks = _kv_chunks(kv_refs, n_kv)
    cols = [slice(LANES * t, LANES * (t + 1)) for t in range(q_ref.shape[2] // LANES)]
    results = _streaming_attention([(_split_heads(q_ref[0, :, c]), chunks) for c in cols])
    for c, (acc, l, _) in zip(cols, results):
        o_ref[0, :, c] = _merge_heads(acc / l, tq).astype(BF16)


def _diff_kernel(*refs, n_kv, lam_init):
    q_ref, kv_refs = refs[0], refs[1:1 + 2 * n_kv]
    lam_ref, gain_ref, o_ref = refs[-3], refs[-2], refs[-1]
    tq = q_ref.shape[1]
    lv = lam_ref[...]
    lam = (jnp.exp(jnp.sum(lv[0:1] * lv[1:2], axis=-1, keepdims=True))
           - jnp.exp(jnp.sum(lv[2:3] * lv[3:4], axis=-1, keepdims=True)) + lam_init)
    cols = [slice(LANES * t, LANES * (t + 1)) for t in range(q_ref.shape[2] // LANES)]
    results = _streaming_attention(
        [(_split_heads(q_ref[0, :, c]), _kv_chunks(kv_refs, n_kv, c)) for c in cols])
    for c, (acc, l, _) in zip(cols, results):
        o2 = acc / l
        o = o2[:tq] - lam * o2[tq:]
        o = _rms_rows(o) * gain_ref[0:1, :] * (1.0 - lam_init)
        o_ref[0, :, c] = o.astype(BF16)


def _window_kernel(q_ref, kc_ref, vc_ref, kl_ref, vl_ref, bias_ref, sink_ref, o_ref, *, band):
    tq = q_ref.shape[1]
    seq = kl_ref.shape[1]
    n_t = q_ref.shape[2] // LANES
    i = pl.program_id(2)
    start = pl.multiple_of(jnp.clip(i * tq - WINDOW, 0, seq - band), LANES)
    problems = []
    for t in range(n_t):
        kv = t // Q_PER_KV_BLOCK
        cols = slice(LANES * kv, LANES * (kv + 1))
        chunks = [(kc_ref, vc_ref, pl.ds(0, kc_ref.shape[1]), cols, None),
                  (kl_ref, vl_ref, pl.ds(start, band), cols, bias_ref)]
        problems.append((_split_heads(q_ref[0, :, LANES * t:LANES * (t + 1)]), chunks))
    lane = lax.broadcasted_iota(jnp.int32, (tq, LANES), 1)
    for t, (acc, l, m) in enumerate(_streaming_attention(problems)):
        head = 2 * (pl.program_id(1) * n_t + t)
        l_lo = l[:tq] + jnp.exp2(sink_ref[pl.ds(head, 1), :] * LOG2E - m[:tq])
        l_hi = l[tq:] + jnp.exp2(sink_ref[pl.ds(head + 1, 1), :] * LOG2E - m[tq:])
        o = jnp.where(lane < HEAD_DIM, acc[:tq] / l_lo, acc[tq:] / l_hi)
        o_ref[0, :, LANES * t:LANES * (t + 1)] = o.astype(BF16)


def _col_spec(rows, col, width):
    if callable(col):
        return pl.BlockSpec((1, rows, width), lambda b, j, i: (b, 0, col(j)))
    return pl.BlockSpec((1, rows, width), lambda b, j, i: (b, 0, col))


def _attn_ab(kind, q_arr, kv_arrs, q_col, k_col, v_col, tq, extra=(), lam_init=0.0, n_t=1):
    nb, t, _ = q_arr.shape
    n_kv = len(kv_arrs)
    qw = n_t * LANES
    kvw = LANES if kind == "gqa" else qw
    n_blocks = 4 // n_t
    in_specs = [pl.BlockSpec((1, tq, qw), lambda b, j, i: (b, i, q_col(j)))]
    in_specs += [_col_spec(a.shape[1], k_col, kvw) for a in kv_arrs]
    in_specs += [_col_spec(a.shape[1], v_col, kvw) for a in kv_arrs]
    in_specs += [pl.BlockSpec(e.shape, lambda b, j, i: (0, 0)) for e in extra]
    if kind == "gqa":
        kern = functools.partial(_gqa_kernel, n_kv=n_kv)
    else:
        kern = functools.partial(_diff_kernel, n_kv=n_kv, lam_init=lam_init)
    return pl.pallas_call(
        kern,
        grid=(nb, n_blocks, t // tq),
        in_specs=in_specs,
        out_specs=pl.BlockSpec((1, tq, qw), lambda b, j, i: (b, i, j)),
        out_shape=jax.ShapeDtypeStruct((nb, t, n_blocks * qw), BF16),
        compiler_params=_params(3),
        name="attn_" + kind,
    )(q_arr, *kv_arrs, *kv_arrs, *extra)


def _attn_window(p_lat, p_ctx, bias, sinkv, tq, band, n_t):
    nb, t, _ = p_lat.shape
    n_q = t // tq
    n_qblocks = sinkv.shape[0] // 2
    qw = n_t * LANES
    n_kv = max(1, n_t // Q_PER_KV_BLOCK)
    kvw = n_kv * LANES
    kcol = lambda j: (j * n_t) // (Q_PER_KV_BLOCK * n_kv)
    case = lambda i: jnp.where(i == 0, 0, jnp.where(i == n_q - 1, 2, 1))
    return pl.pallas_call(
        functools.partial(_window_kernel, band=band),
        grid=(nb, n_qblocks // n_t, n_q),
        in_specs=[pl.BlockSpec((1, tq, qw), lambda b, j, i: (b, i, j)),
                  pl.BlockSpec((1, CTX_LEN, kvw), lambda b, j, i: (b, 0, kcol(j))),
                  pl.BlockSpec((1, CTX_LEN, kvw), lambda b, j, i: (b, 0, 2 // n_kv + kcol(j))),
                  pl.BlockSpec((1, t, kvw), lambda b, j, i: (b, 0, 8 // n_kv + kcol(j))),
                  pl.BlockSpec((1, t, kvw), lambda b, j, i: (b, 0, 10 // n_kv + kcol(j))),
                  pl.BlockSpec((1, 2 * tq, band), lambda b, j, i: (case(i), 0, 0)),
                  pl.BlockSpec(sinkv.shape, lambda b, j, i: (0, 0))],
        out_specs=pl.BlockSpec((1, tq, qw), lambda b, j, i: (b, i, j)),
        out_shape=jax.ShapeDtypeStruct((nb, t, n_qblocks * LANES), BF16),
        compiler_params=_params(3),
        name="attn_window",
    )(p_lat, p_ctx, p_ctx, p_lat, p_lat, bias, sinkv)


def _rope_tables(seq):
    rows = seq // GRID_W
    row = np.repeat(np.arange(rows, dtype=np.float64), GRID_W)
    col = np.tile(np.arange(GRID_W, dtype=np.float64), rows)
    inv = ROPE_THETA ** (-np.arange(0, ROPE_HALF, 2, dtype=np.float64) / ROPE_HALF)
    ang = np.concatenate([row[:, None] * inv, col[:, None] * inv], axis=-1)
    cos, sin = np.cos(ang), np.sin(ang)
    cos128 = np.tile(cos, (1, LANES // ROPE_HALF))
    sin128 = np.concatenate([-sin, -sin, sin, sin], axis=-1)
    return jnp.asarray(cos128, F32), jnp.asarray(sin128, F32)


_INTERLEAVE = np.concatenate([np.arange(0, 32), np.arange(64, 96), np.arange(32, 64), np.arange(96, 128)])


def _interleave_blocks(w, qk_blocks):
    idx = np.arange(w.shape[-1])
    for b in qk_blocks:
        idx[LANES * b:LANES * (b + 1)] = LANES * b + _INTERLEAVE
    return w[..., idx]


def _window_bias(seq, tq, band):
    n_q = seq // tq
    out = np.zeros((3, 2 * tq, band), np.float32)
    for case, i in enumerate((0, 1, n_q - 1)):
        start = min(max(i * tq - WINDOW, 0), seq - band)
        qpos = i * tq + np.arange(tq)[:, None]
        kpos = start + np.arange(band)[None, :]
        m = np.where(np.abs(kpos - qpos) <= WINDOW, 0.0, -np.inf).astype(np.float32)
        out[case] = np.concatenate([m, m], axis=0)
    return jnp.asarray(out)


def _pair_order(n_heads, n_kv):
    g = n_heads // n_kv
    order = []
    for kv in range(0, n_kv, 2):
        for t in range(g):
            order += [kv * g + t, (kv + 1) * g + t]
    return order


def _permute_heads(w, order, axis):
    shape = w.shape
    n = len(order)
    if axis == 1:
        return w.reshape(shape[0], n, HEAD_DIM)[:, np.asarray(order), :].reshape(shape)
    return w.reshape(n, HEAD_DIM, shape[1])[np.asarray(order)].reshape(shape)


def _tile_lanes(v):
    return jnp.tile(v, LANES // v.shape[-1])


def _pad_rows(rows, n=8):
    m = jnp.stack([_tile_lanes(r) if r.shape[-1] != LANES else r for r in rows]).astype(F32)
    return jnp.concatenate([m, jnp.zeros((n - m.shape[0], LANES), F32)], axis=0)


def kernel(x, c, ctx, c_ctx, w_mod, b_mod, g_pre, g_post, w_ffn_gate, w_ffn_up, w_ffn_down,
           w_in_ab, w_out_ab, q_gain_a, k_gain_a, lam_q1, lam_k1, lam_q2, lam_k2, sub_gain_b,
           w_in_c, w_out_c, sink_c):
    nb, seq, d = x.shape
    depth = w_mod.shape[0]
    assert depth == 2, "an odd layer that is not last would also need the windowed mixer's context output"
    n_ctx = ctx.shape[1]
    ctx_row = nb

    cc = jnp.concatenate([c, c_ctx[None, :], jnp.zeros((MOD_ROWS - nb - 1, d), F32)], axis=0)
    mod_all = _modulation(cc, w_mod, b_mod).reshape(depth, MOD_ROWS, 3 * N_SUB, d)

    cos_l, sin_l = _rope_tables(seq)
    scale = HEAD_DIM ** -0.5 * LOG2E

    lat_row = lambda b: b
    ctx_rowf = lambda b: ctx_row
    tm_proj = 512
    tm_ffn = 1024
    cos_c = jnp.ones((tm_proj, LANES), F32)
    sin_c = jnp.zeros((tm_proj, LANES), F32)
    x_lat = x
    x_ctx = ctx.reshape(1, nb * n_ctx, d)
    wg, wu, wd = w_ffn_gate.astype(BF16), w_ffn_up.astype(BF16), w_ffn_down.astype(BF16)

    for i in range(depth):
        last = i == depth - 1
        mod = mod_all[i]
        gpre, gpost = g_pre[i], g_post[i]

        if i % 2 == 0:
            e = i // 2
            lam_init = 0.8 - 0.6 * math.exp(-0.3 * i)
            order_a = _pair_order(A_HEADS, A_KV_HEADS)
            n_qa = A_HEADS * HEAD_DIM
            w_in = w_in_ab[e]
            w_in_lat = jnp.concatenate([_permute_heads(w_in[:, :n_qa], order_a, 1), w_in[:, n_qa:]], axis=1)
            w_in_lat = _interleave_blocks(w_in_lat, [0, 1, 2, 3, 4, 6, 7, 8, 9, 10, 11, 12, 13]).astype(BF16)
            w_in_ctx = w_in_lat
            w_out = w_out_ab[e]
            w_out = jnp.concatenate([_permute_heads(w_out[:n_qa], order_a, 0), w_out[n_qa:]],
                                    axis=0).astype(BF16)
            gains = _interleave_blocks(_pad_rows([q_gain_a[e], k_gain_a[e]]), [0])
            lamv = _pad_rows([jnp.pad(v, (0, LANES - HEAD_DIM))
                              for v in (lam_q1[e], lam_k1[e], lam_q2[e], lam_k2[e])])
            subg = _pad_rows([sub_gain_b[e]])
            blocks = ([(0, True, scale)] * 4 + [(1, True, 1.0), (None, False, 1.0)]
                      + [(None, True, scale)] * 4 + [(None, True, 1.0)] * 4 + [(None, False, 1.0)] * 4)
            blocks_ctx = [(g, False, s) for g, _, s in blocks]
        else:
            o = i // 2
            order_c = _pair_order(C_HEADS, C_KV_HEADS)
            n_qc = C_HEADS * HEAD_DIM
            w_in = w_in_c[o]
            w_in_lat = jnp.concatenate([_permute_heads(w_in[:, :n_qc], order_c, 1), w_in[:, n_qc:]], axis=1)
            w_in_lat = _interleave_blocks(w_in_lat, range(10)).astype(BF16)
            w_in_ctx = w_in_lat[:, n_qc:]
            w_out = _permute_heads(w_out_c[o], order_c, 0).astype(BF16)
            sink = sink_c[o][np.asarray(order_c)].astype(F32)
            sinkv = jnp.repeat(sink, LANES).reshape(C_HEADS, LANES)
            gains = jnp.zeros((8, LANES), F32)
            blocks = [(None, True, scale)] * 8 + [(None, True, 1.0)] * 2 + [(None, False, 1.0)] * 2
            blocks_ctx = [(None, False, 1.0)] * 4

        x_lat = _ffn(x_lat, mod, lat_row, gpre, gpost, wg, wu, wd, (i, 0), 0, tm_ffn)
        x_ctx = _ffn(x_ctx, mod, ctx_rowf, gpre, gpost, wg, wu, wd, (i, 0), 0, tm_ffn)
        p_lat = _proj(x_lat, mod, lat_row, gpre, w_in_lat, cos_l, sin_l, gains, blocks, tm_proj, True)
        p_ctx = _proj(x_ctx, mod, ctx_rowf, gpre, w_in_ctx, cos_c, sin_c, gains, blocks_ctx, tm_proj,
                      False).reshape(nb, n_ctx, -1)

        if i % 2 == 0:
            qa = lambda j: j
            qb, kb, vb = (lambda j: 3 + j), (lambda j: 5 + j), (lambda j: 7 + j)
            a_lat = _attn_ab("gqa", p_lat, [p_ctx, p_lat], qa, 4, 5, 256, n_t=4)
            b_lat = _attn_ab("diff", p_lat, [p_ctx, p_lat], qb, kb, vb, 512, (lamv, subg), lam_init, n_t=2)
            mix_lat = [a_lat, b_lat]
            if not last:
                a_ctx = _attn_ab("gqa", p_ctx, [p_ctx], qa, 4, 5, n_ctx, n_t=4)
                b_ctx = _attn_ab("diff", p_ctx, [p_ctx], qb, kb, vb, n_ctx, (lamv, subg), lam_init, n_t=2)
                mix_ctx = [a_ctx.reshape(1, nb * n_ctx, -1), b_ctx.reshape(1, nb * n_ctx, -1)]
        else:
            tq = 256
            band = tq + 2 * WINDOW
            c_lat = _attn_window(p_lat, p_ctx, _window_bias(seq, tq, band), sinkv, tq, band, 8)
            mix_lat = [c_lat]

        x_lat = _ffn(x_lat, mod, lat_row, gpre, gpost, wg, wu, wd, (i, 1), 2, tm_ffn, mix_lat, w_out)
        if not last:
            x_ctx = _ffn(x_ctx, mod, ctx_rowf, gpre, gpost, wg, wu, wd, (i, 1), 2, tm_ffn, mix_ctx, w_out)
    return x_lat
```

```python
import functools
import math

import numpy as np
import jax
import jax.numpy as jnp
from jax import lax
from jax.experimental import pallas as pl
from jax.experimental.pallas import tpu as pltpu

D_MODEL = 1024
CTX_LEN = 256
GRID_W = 64
HEAD_DIM = 64
ROPE_HALF = HEAD_DIM // 2
ROPE_THETA = 10000.0
A_HEADS = 8
A_KV_HEADS = 2
B_HEADS = 4
C_HEADS = 16
C_KV_HEADS = 4
WINDOW = 128
D_FF = 2816
N_SUB = 3
HALF_STEP = 0.5
EPS = 1e-6

LANES = 128
MOD_ROWS = 24
VMEM_LIMIT = 56 * 1024 * 1024
FFN_ROWS = 256
KEY_CHUNK = 1024
Q_PER_KV_BLOCK = 4
LOG2E = 1.4426950408889634

F32 = jnp.float32
BF16 = jnp.bfloat16
NT_DIMS = (((1,), (1,)), ((), ()))


def _params(n_grid):
    return pltpu.CompilerParams(dimension_semantics=("parallel",) * n_grid,
                                vmem_limit_bytes=VMEM_LIMIT)


def _rms_rows(x):
    return x * lax.rsqrt(jnp.mean(x * x, axis=-1, keepdims=True) + EPS)


def _pre_norm(x, mod_ref, gpre_ref, sub):
    shift = mod_ref[0, 3 * sub + 0:3 * sub + 1, :]
    scale = mod_ref[0, 3 * sub + 1:3 * sub + 2, :]
    return _rms_rows(x) * gpre_ref[sub:sub + 1, :] * (1.0 + scale) + shift


def _post_residual(x, y, mod_ref, gpost_ref, sub, res_w):
    gate = mod_ref[0, 3 * sub + 2:3 * sub + 3, :]
    return x + res_w * gate * (_rms_rows(y) * gpost_ref[sub:sub + 1, :])


def _mod_kernel(c_ref, w_ref, b_ref, o_ref):
    c = c_ref[...]
    act = (c * jax.nn.sigmoid(c)).astype(BF16)
    o_ref[0] = jnp.dot(act, w_ref[0].astype(BF16), preferred_element_type=F32) + b_ref[0]


def _modulation(cc, w_mod, b_mod):
    depth, d, n = w_mod.shape
    tn = 1152
    return pl.pallas_call(
        _mod_kernel,
        grid=(depth, n // tn),
        in_specs=[pl.BlockSpec((MOD_ROWS, d), lambda i, j: (0, 0)),
                  pl.BlockSpec((1, d, tn), lambda i, j: (i, 0, j)),
                  pl.BlockSpec((1, 1, tn), lambda i, j: (i, 0, j))],
        out_specs=pl.BlockSpec((1, MOD_ROWS, tn), lambda i, j: (i, 0, j)),
        out_shape=jax.ShapeDtypeStruct((depth, MOD_ROWS, n), F32),
        compiler_params=_params(2),
        name="adaln_mod",
    )(cc, w_mod, b_mod.reshape(depth, 1, n))


def _mixer_residual(x, a_refs, r, rows, w_ref, mod_ref, gpost_ref):
    y = None
    row = 0
    for a_ref in a_refs:
        k = a_ref.shape[2]
        part = jnp.dot(a_ref[0, r:r + rows, :], w_ref[row:row + k, :], preferred_element_type=F32)
        y = part if y is None else y + part
        row += k
    return _post_residual(x, y, mod_ref, gpost_ref, 1, 1.0)


def _ffn_kernel(*refs, sub, rows, n_mix):
    a_refs = refs[:n_mix]
    x_ref, mod_ref, gpre_ref, gpost_ref = refs[n_mix:n_mix + 4]
    wout_ref = refs[n_mix + 4] if n_mix else None
    wg_ref, wu_ref, wd_ref, o_ref = refs[-4:]
    starts = range(0, x_ref.shape[1], rows)
    xs = [x_ref[0, r:r + rows, :] for r in starts]
    if n_mix:
        xs = [_mixer_residual(x, a_refs, r, rows, wout_ref, mod_ref, gpost_ref) for x, r in zip(xs, starts)]
    for x, r in zip(xs, starts):
        hb = _pre_norm(x, mod_ref, gpre_ref, sub).astype(BF16)
        g = jnp.dot(hb, wg_ref[...], preferred_element_type=F32)
        u = jnp.dot(hb, wu_ref[...], preferred_element_type=F32)
        a = (g * jax.nn.sigmoid(g) * u).astype(BF16)
        y = jnp.dot(a, wd_ref[...], preferred_element_type=F32)
        o_ref[0, r:r + rows, :] = _post_residual(x, y, mod_ref, gpost_ref, sub, HALF_STEP)


def _const_spec(shape):
    return pl.BlockSpec(shape, lambda *_: (0,) * len(shape), pipeline_mode=pl.Buffered(1))


def _slab_spec(arr, lead):
    n = len(lead)
    return pl.BlockSpec((None,) * n + arr.shape[n:], lambda *_: tuple(lead) + (0,) * (arr.ndim - n),
                        pipeline_mode=pl.Buffered(1))


def _ffn(x, mod, mod_row, gpre, gpost, wg, wu, wd, widx, sub, tm, mix=(), w_out=None):
    nb, t, d = x.shape
    kern = functools.partial(_ffn_kernel, sub=sub, rows=FFN_ROWS, n_mix=len(mix))
    row_spec = lambda width: pl.BlockSpec((1, tm, width), lambda b, i: (b, i, 0))
    in_specs = [row_spec(a.shape[2]) for a in mix]
    in_specs += [row_spec(d),
                 pl.BlockSpec((1, 3 * N_SUB, d), lambda b, i: (mod_row(b), 0, 0)),
                 _const_spec(gpre.shape), _const_spec(gpost.shape)]
    in_specs += [_const_spec(w_out.shape)] if mix else []
    in_specs += [_slab_spec(wg, widx), _slab_spec(wu, widx), _slab_spec(wd, widx)]
    return pl.pallas_call(
        kern,
        grid=(nb, t // tm),
        in_specs=in_specs,
        out_specs=row_spec(d),
        out_shape=jax.ShapeDtypeStruct(x.shape, F32),
        compiler_params=_params(2),
        name="swiglu_sublayer",
    )(*mix, x, mod, gpre, gpost, *([w_out] if mix else []), wg, wu, wd)


def _proj_kernel(x_ref, mod_ref, gpre_ref, w_ref, cos_ref, sin_ref, gain_ref, o_ref, *, blocks):
    x = x_ref[0]
    rows = x.shape[0]
    hb = _pre_norm(x, mod_ref, gpre_ref, 1).astype(BF16)
    res = jnp.dot(hb, w_ref[...], preferred_element_type=F32)
    lane = lax.broadcasted_iota(jnp.int32, (rows, LANES), 1)
    lo = _first_head(lane)
    for j, (gain_row, rotary, scale) in enumerate(blocks):
        v = res[:, LANES * j:LANES * (j + 1)]
        if gain_row is not None:
            v2 = v * v
            ms_lo = jnp.sum(jnp.where(lo, v2, 0.0), axis=-1, keepdims=True)
            ms_hi = jnp.sum(jnp.where(lo, 0.0, v2), axis=-1, keepdims=True)
            ms = jnp.where(lo, ms_lo, ms_hi) * (1.0 / HEAD_DIM)
            v = v * lax.rsqrt(ms + EPS) * gain_ref[gain_row:gain_row + 1, :]
        if rotary:
            v = v * cos_ref[...] + pltpu.roll(v, LANES // 2, 1) * sin_ref[...]
        if scale != 1.0:
            v = v * scale
        o_ref[0, :, LANES * j:LANES * (j + 1)] = v.astype(BF16)


def _proj(x, mod, mod_row, gpre, w, cos, sin, gains, blocks, tm, rope_rows):
    nb, t, d = x.shape
    n = w.shape[1]
    kern = functools.partial(_proj_kernel, blocks=tuple(blocks))
    rope_map = (lambda b, i: (i, 0)) if rope_rows else (lambda b, i: (0, 0))
    return pl.pallas_call(
        kern,
        grid=(nb, t // tm),
        in_specs=[pl.BlockSpec((1, tm, d), lambda b, i: (b, i, 0)),
                  pl.BlockSpec((1, 3 * N_SUB, d), lambda b, i: (mod_row(b), 0, 0)),
                  _const_spec(gpre.shape), _const_spec(w.shape),
                  pl.BlockSpec((tm, LANES), rope_map),
                  pl.BlockSpec((tm, LANES), rope_map),
                  _const_spec(gains.shape)],
        out_specs=pl.BlockSpec((1, tm, n), lambda b, i: (b, i, 0)),
        out_shape=jax.ShapeDtypeStruct((nb, t, n), BF16),
        compiler_params=_params(2),
        name="qkv_proj",
    )(x, mod, gpre, w, cos, sin, gains)


def _first_head(lane):
    return (lane & ROPE_HALF) == 0


def _split_heads(q):
    first = _first_head(lax.broadcasted_iota(jnp.int32, q.shape, 1))
    zero = jnp.zeros_like(q)
    return jnp.concatenate([jnp.where(first, q, zero), jnp.where(first, zero, q)], axis=0)


def _kv_chunks(kv_refs, n_kv, cols=slice(0, LANES)):
    chunks = []
    for k_ref, v_ref in zip(kv_refs[:n_kv], kv_refs[n_kv:]):
        n = k_ref.shape[1]
        step = min(n, KEY_CHUNK)
        chunks += [(k_ref, v_ref, pl.ds(a, step), cols, None) for a in range(0, n, step)]
    return chunks


def _streaming_attention(problems):
    items = [(pi, ci) for pi, (_, chunks) in enumerate(problems) for ci in range(len(chunks))]
    state = [[None, None] for _ in problems]
    scores, probs = {}, {}

    def stage_scores(w):
        pi, ci = items[w]
        qq, chunks = problems[pi]
        k_ref, _, rows, cols, bias_ref = chunks[ci]
        s = lax.dot_general(qq, k_ref[0, rows, cols], NT_DIMS, preferred_element_type=F32)
        scores[w] = s if bias_ref is None else s + bias_ref[0]

    def stage_exp(w):
        pi, _ = items[w]
        m = state[pi][0]
        s = scores.pop(w)
        s_max = jnp.max(s, axis=-1, keepdims=True)
        m_new = s_max if m is None else jnp.maximum(m, s_max)
        alpha = None if m is None else jnp.exp2(m - m_new)
        state[pi][0] = m_new
        probs[w] = (jnp.exp2(s - m_new).astype(BF16), alpha)

    def stage_values(w):
        pi, ci = items[w]
        _, v_ref, rows, cols, _ = problems[pi][1][ci]
        p, alpha = probs.pop(w)
        v = v_ref[0, rows, cols]
        pv = jnp.dot(p, jnp.concatenate([v, jnp.ones_like(v)], axis=1), preferred_element_type=F32)
        state[pi][1] = pv if alpha is None else alpha * state[pi][1] + pv

    n = len(items)
    for step in range(n + 2):
        if step < n:
            stage_scores(step)
        if 0 <= step - 1 < n:
            stage_exp(step - 1)
        if 0 <= step - 2 < n:
            stage_values(step - 2)
    return [(acc[:, :LANES], acc[:, LANES:], m) for m, acc in state]


def _merge_heads(o2, tq):
    lane = lax.broadcasted_iota(jnp.int32, (tq, LANES), 1)
    return jnp.where(lane < HEAD_DIM, o2[:tq], o2[tq:])


def _gqa_kernel(*refs, n_kv):
    q_ref, kv_refs, o_ref = refs[0], refs[1:1 + 2 * n_kv], refs[-1]
    tq = q_ref.shape[1]
    chunks = _kv_chunks(kv_refs, n_kv)
    cols = [slice(LANES * t, LANES * (t + 1)) for t in range(q_ref.shape[2] // LANES)]
    results = _streaming_attention([(_split_heads(q_ref[0, :, c]), chunks) for c in cols])
    for c, (acc, l, _) in zip(cols, results):
        o_ref[0, :, c] = _merge_heads(acc / l, tq).astype(BF16)


def _diff_kernel(*refs, n_kv, lam_init):
    q_ref, kv_refs = refs[0], refs[1:1 + 2 * n_kv]
    lam_ref, gain_ref, o_ref = refs[-3], refs[-2], refs[-1]
    tq = q_ref.shape[1]
    lv = lam_ref[...]
    lam = (jnp.exp(jnp.sum(lv[0:1] * lv[1:2], axis=-1, keepdims=True))
           - jnp.exp(jnp.sum(lv[2:3] * lv[3:4], axis=-1, keepdims=True)) + lam_init)
    cols = [slice(LANES * t, LANES * (t + 1)) for t in range(q_ref.shape[2] // LANES)]
    results = _streaming_attention(
        [(_split_heads(q_ref[0, :, c]), _kv_chunks(kv_refs, n_kv, c)) for c in cols])
    for c, (acc, l, _) in zip(cols, results):
        o2 = acc / l
        o = o2[:tq] - lam * o2[tq:]
        o = _rms_rows(o) * gain_ref[0:1, :] * (1.0 - lam_init)
        o_ref[0, :, c] = o.astype(BF16)


def _window_kernel(q_ref, kc_ref, vc_ref, kl_ref, vl_ref, bias_ref, sink_ref, o_ref, *, band):
    tq = q_ref.shape[1]
    seq = kl_ref.shape[1]
    n_t = q_ref.shape[2] // LANES
    i = pl.program_id(2)
    start = pl.multiple_of(jnp.clip(i * tq - WINDOW, 0, seq - band), LANES)
    problems = []
    for t in range(n_t):
        kv = t // Q_PER_KV_BLOCK
        cols = slice(LANES * kv, LANES * (kv + 1))
        chunks = [(kc_ref, vc_ref, pl.ds(0, kc_ref.shape[1]), cols, None),
                  (kl_ref, vl_ref, pl.ds(start, band), cols, bias_ref)]
        problems.append((_split_heads(q_ref[0, :, LANES * t:LANES * (t + 1)]), chunks))
    lane = lax.broadcasted_iota(jnp.int32, (tq, LANES), 1)
    for t, (acc, l, m) in enumerate(_streaming_attention(problems)):
        head = 2 * (pl.program_id(1) * n_t + t)
        l_lo = l[:tq] + jnp.exp2(sink_ref[pl.ds(head, 1), :] * LOG2E - m[:tq])
        l_hi = l[tq:] + jnp.exp2(sink_ref[pl.ds(head + 1, 1), :] * LOG2E - m[tq:])
        o = jnp.where(lane < HEAD_DIM, acc[:tq] / l_lo, acc[tq:] / l_hi)
        o_ref[0, :, LANES * t:LANES * (t + 1)] = o.astype(BF16)


def _col_spec(rows, col, width):
    if callable(col):
        return pl.BlockSpec((1, rows, width), lambda b, j, i: (b, 0, col(j)))
    return pl.BlockSpec((1, rows, width), lambda b, j, i: (b, 0, col))


def _attn_ab(kind, q_arr, kv_arrs, q_col, k_col, v_col, tq, extra=(), lam_init=0.0, n_t=1):
    nb, t, _ = q_arr.shape
    n_kv = len(kv_arrs)
    qw = n_t * LANES
    kvw = LANES if kind == "gqa" else qw
    n_blocks = 4 // n_t
    in_specs = [pl.BlockSpec((1, tq, qw), lambda b, j, i: (b, i, q_col(j)))]
    in_specs += [_col_spec(a.shape[1], k_col, kvw) for a in kv_arrs]
    in_specs += [_col_spec(a.shape[1], v_col, kvw) for a in kv_arrs]
    in_specs += [pl.BlockSpec(e.shape, lambda b, j, i: (0, 0)) for e in extra]
    if kind == "gqa":
        kern = functools.partial(_gqa_kernel, n_kv=n_kv)
    else:
        kern = functools.partial(_diff_kernel, n_kv=n_kv, lam_init=lam_init)
    return pl.pallas_call(
        kern,
        grid=(nb, n_blocks, t // tq),
        in_specs=in_specs,
        out_specs=pl.BlockSpec((1, tq, qw), lambda b, j, i: (b, i, j)),
        out_shape=jax.ShapeDtypeStruct((nb, t, n_blocks * qw), BF16),
        compiler_params=_params(3),
        name="attn_" + kind,
    )(q_arr, *kv_arrs, *kv_arrs, *extra)


def _attn_window(p_lat, p_ctx, bias, sinkv, tq, band, n_t):
    nb, t, _ = p_lat.shape
    n_q = t // tq
    n_qblocks = sinkv.shape[0] // 2
    qw = n_t * LANES
    n_kv = max(1, n_t // Q_PER_KV_BLOCK)
    kvw = n_kv * LANES
    kcol = lambda j: (j * n_t) // (Q_PER_KV_BLOCK * n_kv)
    case = lambda i: jnp.where(i == 0, 0, jnp.where(i == n_q - 1, 2, 1))
    return pl.pallas_call(
        functools.partial(_window_kernel, band=band),
        grid=(nb, n_qblocks // n_t, n_q),
        in_specs=[pl.BlockSpec((1, tq, qw), lambda b, j, i: (b, i, j)),
                  pl.BlockSpec((1, CTX_LEN, kvw), lambda b, j, i: (b, 0, kcol(j))),
                  pl.BlockSpec((1, CTX_LEN, kvw), lambda b, j, i: (b, 0, 2 // n_kv + kcol(j))),
                  pl.BlockSpec((1, t, kvw), lambda b, j, i: (b, 0, 8 // n_kv + kcol(j))),
                  pl.BlockSpec((1, t, kvw), lambda b, j, i: (b, 0, 10 // n_kv + kcol(j))),
                  pl.BlockSpec((1, 2 * tq, band), lambda b, j, i: (case(i), 0, 0)),
                  pl.BlockSpec(sinkv.shape, lambda b, j, i: (0, 0))],
        out_specs=pl.BlockSpec((1, tq, qw), lambda b, j, i: (b, i, j)),
        out_shape=jax.ShapeDtypeStruct((nb, t, n_qblocks * LANES), BF16),
        compiler_params=_params(3),
        name="attn_window",
    )(p_lat, p_ctx, p_ctx, p_lat, p_lat, bias, sinkv)


def _rope_tables(seq):
    rows = seq // GRID_W
    row = np.repeat(np.arange(rows, dtype=np.float64), GRID_W)
    col = np.tile(np.arange(GRID_W, dtype=np.float64), rows)
    inv = ROPE_THETA ** (-np.arange(0, ROPE_HALF, 2, dtype=np.float64) / ROPE_HALF)
    ang = np.concatenate([row[:, None] * inv, col[:, None] * inv], axis=-1)
    cos, sin = np.cos(ang), np.sin(ang)
    cos128 = np.tile(cos, (1, LANES // ROPE_HALF))
    sin128 = np.concatenate([-sin, -sin, sin, sin], axis=-1)
    return jnp.asarray(cos128, F32), jnp.asarray(sin128, F32)


_INTERLEAVE = np.concatenate([np.arange(0, 32), np.arange(64, 96), np.arange(32, 64), np.arange(96, 128)])


def _interleave_blocks(w, qk_blocks):
    idx = np.arange(w.shape[-1])
    for b in qk_blocks:
        idx[LANES * b:LANES * (b + 1)] = LANES * b + _INTERLEAVE
    return w[..., idx]


def _window_bias(seq, tq, band):
    n_q = seq // tq
    out = np.zeros((3, 2 * tq, band), np.float32)
    for case, i in enumerate((0, 1, n_q - 1)):
        start = min(max(i * tq - WINDOW, 0), seq - band)
        qpos = i * tq + np.arange(tq)[:, None]
        kpos = start + np.arange(band)[None, :]
        m = np.where(np.abs(kpos - qpos) <= WINDOW, 0.0, -np.inf).astype(np.float32)
        out[case] = np.concatenate([m, m], axis=0)
    return jnp.asarray(out)


def _pair_order(n_heads, n_kv):
    g = n_heads // n_kv
    order = []
    for kv in range(0, n_kv, 2):
        for t in range(g):
            order += [kv * g + t, (kv + 1) * g + t]
    return order


def _permute_heads(w, order, axis):
    shape = w.shape
    n = len(order)
    if axis == 1:
        return w.reshape(shape[0], n, HEAD_DIM)[:, np.asarray(order), :].reshape(shape)
    return w.reshape(n, HEAD_DIM, shape[1])[np.asarray(order)].reshape(shape)


def _tile_lanes(v):
    return jnp.tile(v, LANES // v.shape[-1])


def _pad_rows(rows, n=8):
    m = jnp.stack([_tile_lanes(r) if r.shape[-1] != LANES else r for r in rows]).astype(F32)
    return jnp.concatenate([m, jnp.zeros((n - m.shape[0], LANES), F32)], axis=0)


def kernel(x, c, ctx, c_ctx, w_mod, b_mod, g_pre, g_post, w_ffn_gate, w_ffn_up, w_ffn_down,
           w_in_ab, w_out_ab, q_gain_a, k_gain_a, lam_q1, lam_k1, lam_q2, lam_k2, sub_gain_b,
           w_in_c, w_out_c, sink_c):
    nb, seq, d = x.shape
    depth = w_mod.shape[0]
    assert depth == 2, "an odd layer that is not last would also need the windowed mixer's context output"
    n_ctx = ctx.shape[1]
    ctx_row = nb

    cc = jnp.concatenate([c, c_ctx[None, :], jnp.zeros((MOD_ROWS - nb - 1, d), F32)], axis=0)
    mod_all = _modulation(cc, w_mod, b_mod).reshape(depth, MOD_ROWS, 3 * N_SUB, d)

    cos_l, sin_l = _rope_tables(seq)
    scale = HEAD_DIM ** -0.5 * LOG2E

    lat_row = lambda b: b
    ctx_rowf = lambda b: ctx_row
    tm_proj = 512
    tm_ffn = 1024
    cos_c = jnp.ones((tm_proj, LANES), F32)
    sin_c = jnp.zeros((tm_proj, LANES), F32)
    x_lat = x
    x_ctx = ctx.reshape(1, nb * n_ctx, d)
    wg, wu, wd = w_ffn_gate.astype(BF16), w_ffn_up.astype(BF16), w_ffn_down.astype(BF16)

    for i in range(depth):
        last = i == depth - 1
        mod = mod_all[i]
        gpre, gpost = g_pre[i], g_post[i]

        if i % 2 == 0:
            e = i // 2
            lam_init = 0.8 - 0.6 * math.exp(-0.3 * i)
            order_a = _pair_order(A_HEADS, A_KV_HEADS)
            n_qa = A_HEADS * HEAD_DIM
            w_in = w_in_ab[e]
            n_kva = 2 * A_KV_HEADS * HEAD_DIM
            w_in_lat = jnp.concatenate([_permute_heads(w_in[:, :n_qa], order_a, 1), w_in[:, n_qa + n_kva:],
                                        w_in[:, n_qa:n_qa + n_kva]], axis=1)
            w_in_lat = _interleave_blocks(w_in_lat, list(range(12)) + [16]).astype(BF16)
            w_in_ctx = w_in_lat
            w_out = w_out_ab[e]
            w_out = jnp.concatenate([_permute_heads(w_out[:n_qa], order_a, 0), w_out[n_qa:]],
                                    axis=0).astype(BF16)
            gains = _interleave_blocks(_pad_rows([q_gain_a[e], k_gain_a[e]]), [0])
            lamv = _pad_rows([jnp.pad(v, (0, LANES - HEAD_DIM))
                              for v in (lam_q1[e], lam_k1[e], lam_q2[e], lam_k2[e])])
            subg = _pad_rows([sub_gain_b[e]])
            blocks = ([(0, True, scale)] * 4 + [(None, True, scale)] * 4 + [(None, True, 1.0)] * 4
                      + [(None, False, 1.0)] * 4 + [(1, True, 1.0), (None, False, 1.0)])
            blocks_ctx = [(g, False, s) for g, _, s in blocks]
        else:
            o = i // 2
            order_c = _pair_order(C_HEADS, C_KV_HEADS)
            n_qc = C_HEADS * HEAD_DIM
            w_in = w_in_c[o]
            w_in_lat = jnp.concatenate([_permute_heads(w_in[:, :n_qc], order_c, 1), w_in[:, n_qc:]], axis=1)
            w_in_lat = _interleave_blocks(w_in_lat, range(10)).astype(BF16)
            w_in_ctx = w_in_lat[:, n_qc:]
            w_out = _permute_heads(w_out_c[o], order_c, 0).astype(BF16)
            sink = sink_c[o][np.asarray(order_c)].astype(F32)
            sinkv = jnp.repeat(sink, LANES).reshape(C_HEADS, LANES)
            gains = jnp.zeros((8, LANES), F32)
            blocks = [(None, True, scale)] * 8 + [(None, True, 1.0)] * 2 + [(None, False, 1.0)] * 2
            blocks_ctx = [(None, False, 1.0)] * 4

        x_lat = _ffn(x_lat, mod, lat_row, gpre, gpost, wg, wu, wd, (i, 0), 0, tm_ffn)
        x_ctx = _ffn(x_ctx, mod, ctx_rowf, gpre, gpost, wg, wu, wd, (i, 0), 0, tm_ffn)
        p_lat = _proj(x_lat, mod, lat_row, gpre, w_in_lat, cos_l, sin_l, gains, blocks, tm_proj, True)
        p_ctx = _proj(x_ctx, mod, ctx_rowf, gpre, w_in_ctx, cos_c, sin_c, gains, blocks_ctx, tm_proj,
                      False).reshape(nb, n_ctx, -1)

        if i % 2 == 0:
            qa, ka, va = (lambda j: 0), 16, 17
            qb, kb, vb = (lambda j: 1), (lambda j: 2), (lambda j: 3)
            a_lat = _attn_ab("gqa", p_lat, [p_ctx, p_lat], qa, ka, va, 256, n_t=4)
            b_lat = _attn_ab("diff", p_lat, [p_ctx, p_lat], qb, kb, vb, 256, (lamv, subg), lam_init, n_t=4)
            mix_lat = [a_lat, b_lat]
            if not last:
                a_ctx = _attn_ab("gqa", p_ctx, [p_ctx], qa, ka, va, n_ctx, n_t=4)
                b_ctx = _attn_ab("diff", p_ctx, [p_ctx], qb, kb, vb, n_ctx, (lamv, subg), lam_init, n_t=4)
                mix_ctx = [a_ctx.reshape(1, nb * n_ctx, -1), b_ctx.reshape(1, nb * n_ctx, -1)]
        else:
            tq = 256
            band = tq + 2 * WINDOW
            c_lat = _attn_window(p_lat, p_ctx, _window_bias(seq, tq, band), sinkv, tq, band, 8)
            mix_lat = [c_lat]

        x_lat = _ffn(x_lat, mod, lat_row, gpre, gpost, wg, wu, wd, (i, 1), 2, tm_ffn, mix_lat, w_out)
        if not last:
            x_ctx = _ffn(x_ctx, mod, ctx_rowf, gpre, gpost, wg, wu, wd, (i, 1), 2, tm_ffn, mix_ctx, w_out)
    return x_lat
```

```python
import functools
import math

import numpy as np
import jax
import jax.numpy as jnp
from jax import lax
from jax.experimental import pallas as pl
from jax.experimental.pallas import tpu as pltpu

D_MODEL = 1024
CTX_LEN = 256
GRID_W = 64
HEAD_DIM = 64
ROPE_HALF = HEAD_DIM // 2
ROPE_THETA = 10000.0
A_HEADS = 8
A_KV_HEADS = 2
B_HEADS = 4
C_HEADS = 16
C_KV_HEADS = 4
WINDOW = 128
D_FF = 2816
N_SUB = 3
HALF_STEP = 0.5
EPS = 1e-6

LANES = 128
MOD_ROWS = 24
VMEM_LIMIT = 56 * 1024 * 1024
FFN_ROWS = 256
Q_PER_KV_BLOCK = 4
LOG2E = 1.4426950408889634

F32 = jnp.float32
BF16 = jnp.bfloat16
NT_DIMS = (((1,), (1,)), ((), ()))


def _params(n_grid):
    return pltpu.CompilerParams(dimension_semantics=("parallel",) * n_grid,
                                vmem_limit_bytes=VMEM_LIMIT)


def _rms_rows(x):
    return x * lax.rsqrt(jnp.mean(x * x, axis=-1, keepdims=True) + EPS)


def _pre_norm(x, mod_ref, gpre_ref, sub):
    shift = mod_ref[0, 3 * sub + 0:3 * sub + 1, :]
    scale = mod_ref[0, 3 * sub + 1:3 * sub + 2, :]
    return _rms_rows(x) * gpre_ref[sub:sub + 1, :] * (1.0 + scale) + shift


def _post_residual(x, y, mod_ref, gpost_ref, sub, res_w):
    gate = mod_ref[0, 3 * sub + 2:3 * sub + 3, :]
    return x + res_w * gate * (_rms_rows(y) * gpost_ref[sub:sub + 1, :])


def _mod_kernel(c_ref, w_ref, b_ref, o_ref):
    c = c_ref[...]
    act = (c * jax.nn.sigmoid(c)).astype(BF16)
    o_ref[0] = jnp.dot(act, w_ref[0].astype(BF16), preferred_element_type=F32) + b_ref[0]


def _modulation(cc, w_mod, b_mod):
    depth, d, n = w_mod.shape
    tn = 1152
    return pl.pallas_call(
        _mod_kernel,
        grid=(depth, n // tn),
        in_specs=[pl.BlockSpec((MOD_ROWS, d), lambda i, j: (0, 0)),
                  pl.BlockSpec((1, d, tn), lambda i, j: (i, 0, j)),
                  pl.BlockSpec((1, 1, tn), lambda i, j: (i, 0, j))],
        out_specs=pl.BlockSpec((1, MOD_ROWS, tn), lambda i, j: (i, 0, j)),
        out_shape=jax.ShapeDtypeStruct((depth, MOD_ROWS, n), F32),
        compiler_params=_params(2),
        name="adaln_mod",
    )(cc, w_mod, b_mod.reshape(depth, 1, n))


def _mixer_residual(x, a_refs, r, rows, w_ref, mod_ref, gpost_ref):
    y = None
    row = 0
    for a_ref in a_refs:
        k = a_ref.shape[2]
        part = jnp.dot(a_ref[0, r:r + rows, :], w_ref[row:row + k, :], preferred_element_type=F32)
        y = part if y is None else y + part
        row += k
    return _post_residual(x, y, mod_ref, gpost_ref, 1, 1.0)


def _ffn_kernel(*refs, sub, rows, n_mix):
    a_refs = refs[:n_mix]
    x_ref, mod_ref, gpre_ref, gpost_ref = refs[n_mix:n_mix + 4]
    wout_ref = refs[n_mix + 4] if n_mix else None
    wg_ref, wu_ref, wd_ref, o_ref = refs[-4:]
    starts = range(0, x_ref.shape[1], rows)
    xs = [x_ref[0, r:r + rows, :] for r in starts]
    if n_mix:
        xs = [_mixer_residual(x, a_refs, r, rows, wout_ref, mod_ref, gpost_ref) for x, r in zip(xs, starts)]
    for x, r in zip(xs, starts):
        hb = _pre_norm(x, mod_ref, gpre_ref, sub).astype(BF16)
        g = jnp.dot(hb, wg_ref[...], preferred_element_type=F32)
        u = jnp.dot(hb, wu_ref[...], preferred_element_type=F32)
        a = (g * jax.nn.sigmoid(g) * u).astype(BF16)
        y = jnp.dot(a, wd_ref[...], preferred_element_type=F32)
        o_ref[0, r:r + rows, :] = _post_residual(x, y, mod_ref, gpost_ref, sub, HALF_STEP)


def _const_spec(shape):
    return pl.BlockSpec(shape, lambda *_: (0,) * len(shape), pipeline_mode=pl.Buffered(1))


def _slab_spec(arr, lead):
    n = len(lead)
    return pl.BlockSpec((None,) * n + arr.shape[n:], lambda *_: tuple(lead) + (0,) * (arr.ndim - n),
                        pipeline_mode=pl.Buffered(1))


def _ffn(x, mod, mod_row, gpre, gpost, wg, wu, wd, widx, sub, tm, mix=(), w_out=None):
    nb, t, d = x.shape
    kern = functools.partial(_ffn_kernel, sub=sub, rows=FFN_ROWS, n_mix=len(mix))
    row_spec = lambda width: pl.BlockSpec((1, tm, width), lambda b, i: (b, i, 0))
    in_specs = [row_spec(a.shape[2]) for a in mix]
    in_specs += [row_spec(d),
                 pl.BlockSpec((1, 3 * N_SUB, d), lambda b, i: (mod_row(b), 0, 0)),
                 _const_spec(gpre.shape), _const_spec(gpost.shape)]
    in_specs += [_const_spec(w_out.shape)] if mix else []
    in_specs += [_slab_spec(wg, widx), _slab_spec(wu, widx), _slab_spec(wd, widx)]
    return pl.pallas_call(
        kern,
        grid=(nb, t // tm),
        in_specs=in_specs,
        out_specs=row_spec(d),
        out_shape=jax.ShapeDtypeStruct(x.shape, F32),
        compiler_params=_params(2),
        name="swiglu_sublayer",
    )(*mix, x, mod, gpre, gpost, *([w_out] if mix else []), wg, wu, wd)


def _proj_kernel(x_ref, mod_ref, gpre_ref, w_ref, cos_ref, sin_ref, gain_ref, o_ref, *, blocks):
    x = x_ref[0]
    rows = x.shape[0]
    hb = _pre_norm(x, mod_ref, gpre_ref, 1).astype(BF16)
    res = jnp.dot(hb, w_ref[...], preferred_element_type=F32)
    lane = lax.broadcasted_iota(jnp.int32, (rows, LANES), 1)
    lo = _first_head(lane)
    for j, (gain_row, rotary, scale) in enumerate(blocks):
        v = res[:, LANES * j:LANES * (j + 1)]
        if gain_row is not None:
            v2 = v * v
            ms_lo = jnp.sum(jnp.where(lo, v2, 0.0), axis=-1, keepdims=True)
            ms_hi = jnp.sum(jnp.where(lo, 0.0, v2), axis=-1, keepdims=True)
            ms = jnp.where(lo, ms_lo, ms_hi) * (1.0 / HEAD_DIM)
            v = v * lax.rsqrt(ms + EPS) * gain_ref[gain_row:gain_row + 1, :]
        if rotary:
            v = v * cos_ref[...] + pltpu.roll(v, LANES // 2, 1) * sin_ref[...]
        if scale != 1.0:
            v = v * scale
        o_ref[0, :, LANES * j:LANES * (j + 1)] = v.astype(BF16)


def _proj(x, mod, mod_row, gpre, w, cos, sin, gains, blocks, tm, rope_rows):
    nb, t, d = x.shape
    n = w.shape[1]
    kern = functools.partial(_proj_kernel, blocks=tuple(blocks))
    rope_map = (lambda b, i: (i, 0)) if rope_rows else (lambda b, i: (0, 0))
    return pl.pallas_call(
        kern,
        grid=(nb, t // tm),
        in_specs=[pl.BlockSpec((1, tm, d), lambda b, i: (b, i, 0)),
                  pl.BlockSpec((1, 3 * N_SUB, d), lambda b, i: (mod_row(b), 0, 0)),
                  _const_spec(gpre.shape), _const_spec(w.shape),
                  pl.BlockSpec((tm, LANES), rope_map),
                  pl.BlockSpec((tm, LANES), rope_map),
                  _const_spec(gains.shape)],
        out_specs=pl.BlockSpec((1, tm, n), lambda b, i: (b, i, 0)),
        out_shape=jax.ShapeDtypeStruct((nb, t, n), BF16),
        compiler_params=_params(2),
        name="qkv_proj",
    )(x, mod, gpre, w, cos, sin, gains)


def _first_head(lane):
    return (lane & ROPE_HALF) == 0


def _split_heads(q):
    first = _first_head(lax.broadcasted_iota(jnp.int32, q.shape, 1))
    zero = jnp.zeros_like(q)
    return jnp.concatenate([jnp.where(first, q, zero), jnp.where(first, zero, q)], axis=0)


def _key_sets(kv_refs, n_kv, cols=slice(0, LANES)):
    return [(k_ref, v_ref, pl.ds(0, k_ref.shape[1]), cols, None)
            for k_ref, v_ref in zip(kv_refs[:n_kv], kv_refs[n_kv:])]


def _attention(problems):
    scores, probs, results = {}, {}, {}

    def stage_scores(i):
        qq, key_sets = problems[i]
        parts = []
        for k_ref, _, rows, cols, bias_ref in key_sets:
            s = lax.dot_general(qq, k_ref[0, rows, cols], NT_DIMS, preferred_element_type=F32)
            parts.append(s if bias_ref is None else s + bias_ref[0])
        scores[i] = jnp.concatenate(parts, axis=1)

    def stage_exp(i):
        s = scores.pop(i)
        m = jnp.max(s, axis=-1, keepdims=True)
        probs[i] = (jnp.exp2(s - m).astype(BF16), m)

    def stage_values(i):
        p, m = probs.pop(i)
        acc, a = None, 0
        for _, v_ref, rows, cols, _ in problems[i][1]:
            v = v_ref[0, rows, cols]
            n = v.shape[0]
            pv = jnp.dot(p[:, a:a + n], jnp.concatenate([v, jnp.ones_like(v)], axis=1),
                         preferred_element_type=F32)
            acc = pv if acc is None else acc + pv
            a += n
        results[i] = (acc[:, :LANES], acc[:, LANES:], m)

    n = len(problems)
    for step in range(n + 2):
        if step < n:
            stage_scores(step)
        if 0 <= step - 1 < n:
            stage_exp(step - 1)
        if 0 <= step - 2 < n:
            stage_values(step - 2)
    return [results[i] for i in range(n)]


def _merge_heads(o2, tq):
    lane = lax.broadcasted_iota(jnp.int32, (tq, LANES), 1)
    return jnp.where(lane < HEAD_DIM, o2[:tq], o2[tq:])


def _gqa_kernel(*refs, n_kv):
    q_ref, kv_refs, o_ref = refs[0], refs[1:1 + 2 * n_kv], refs[-1]
    tq = q_ref.shape[1]
    key_sets = _key_sets(kv_refs, n_kv)
    cols = [slice(LANES * t, LANES * (t + 1)) for t in range(q_ref.shape[2] // LANES)]
    results = _attention([(_split_heads(q_ref[0, :, c]), key_sets) for c in cols])
    for c, (acc, l, _) in zip(cols, results):
        o_ref[0, :, c] = _merge_heads(acc / l, tq).astype(BF16)


def _diff_kernel(*refs, n_kv, lam_init):
    q_ref, kv_refs = refs[0], refs[1:1 + 2 * n_kv]
    lam_ref, gain_ref, o_ref = refs[-3], refs[-2], refs[-1]
    tq = q_ref.shape[1]
    lv = lam_ref[...]
    lam = (jnp.exp(jnp.sum(lv[0:1] * lv[1:2], axis=-1, keepdims=True))
           - jnp.exp(jnp.sum(lv[2:3] * lv[3:4], axis=-1, keepdims=True)) + lam_init)
    cols = [slice(LANES * t, LANES * (t + 1)) for t in range(q_ref.shape[2] // LANES)]
    results = _attention([(_split_heads(q_ref[0, :, c]), _key_sets(kv_refs, n_kv, c)) for c in cols])
    for c, (acc, l, _) in zip(cols, results):
        o2 = acc / l
        o = o2[:tq] - lam * o2[tq:]
        o = _rms_rows(o) * gain_ref[0:1, :] * (1.0 - lam_init)
        o_ref[0, :, c] = o.astype(BF16)


def _window_kernel(q_ref, kc_ref, vc_ref, kl_ref, vl_ref, bias_ref, sink_ref, o_ref, *, band):
    tq = q_ref.shape[1]
    seq = kl_ref.shape[1]
    n_t = q_ref.shape[2] // LANES
    i = pl.program_id(2)
    start = pl.multiple_of(jnp.clip(i * tq - WINDOW, 0, seq - band), LANES)
    problems = []
    for t in range(n_t):
        kv = t // Q_PER_KV_BLOCK
        cols = slice(LANES * kv, LANES * (kv + 1))
        key_sets = [(kc_ref, vc_ref, pl.ds(0, kc_ref.shape[1]), cols, None),
                    (kl_ref, vl_ref, pl.ds(start, band), cols, bias_ref)]
        problems.append((_split_heads(q_ref[0, :, LANES * t:LANES * (t + 1)]), key_sets))
    lane = lax.broadcasted_iota(jnp.int32, (tq, LANES), 1)
    for t, (acc, l, m) in enumerate(_attention(problems)):
        head = 2 * (pl.program_id(1) * n_t + t)
        l_lo = l[:tq] + jnp.exp2(sink_ref[pl.ds(head, 1), :] * LOG2E - m[:tq])
        l_hi = l[tq:] + jnp.exp2(sink_ref[pl.ds(head + 1, 1), :] * LOG2E - m[tq:])
        o = jnp.where(lane < HEAD_DIM, acc[:tq] / l_lo, acc[tq:] / l_hi)
        o_ref[0, :, LANES * t:LANES * (t + 1)] = o.astype(BF16)


def _col_spec(rows, col, width):
    if callable(col):
        return pl.BlockSpec((1, rows, width), lambda b, j, i: (b, 0, col(j)))
    return pl.BlockSpec((1, rows, width), lambda b, j, i: (b, 0, col))


def _attn_ab(kind, q_arr, kv_arrs, q_col, k_col, v_col, tq, extra=(), lam_init=0.0, n_t=1):
    nb, t, _ = q_arr.shape
    n_kv = len(kv_arrs)
    qw = n_t * LANES
    kvw = LANES if kind == "gqa" else qw
    n_blocks = 4 // n_t
    in_specs = [pl.BlockSpec((1, tq, qw), lambda b, j, i: (b, i, q_col(j)))]
    in_specs += [_col_spec(a.shape[1], k_col, kvw) for a in kv_arrs]
    in_specs += [_col_spec(a.shape[1], v_col, kvw) for a in kv_arrs]
    in_specs += [pl.BlockSpec(e.shape, lambda b, j, i: (0, 0)) for e in extra]
    if kind == "gqa":
        kern = functools.partial(_gqa_kernel, n_kv=n_kv)
    else:
        kern = functools.partial(_diff_kernel, n_kv=n_kv, lam_init=lam_init)
    return pl.pallas_call(
        kern,
        grid=(nb, n_blocks, t // tq),
        in_specs=in_specs,
        out_specs=pl.BlockSpec((1, tq, qw), lambda b, j, i: (b, i, j)),
        out_shape=jax.ShapeDtypeStruct((nb, t, n_blocks * qw), BF16),
        compiler_params=_params(3),
        name="attn_" + kind,
    )(q_arr, *kv_arrs, *kv_arrs, *extra)


def _attn_window(p_lat, p_ctx, bias, sinkv, tq, band, n_t):
    nb, t, _ = p_lat.shape
    n_q = t // tq
    n_qblocks = sinkv.shape[0] // 2
    qw = n_t * LANES
    n_kv = max(1, n_t // Q_PER_KV_BLOCK)
    kvw = n_kv * LANES
    kcol = lambda j: (j * n_t) // (Q_PER_KV_BLOCK * n_kv)
    case = lambda i: jnp.where(i == 0, 0, jnp.where(i == n_q - 1, 2, 1))
    return pl.pallas_call(
        functools.partial(_window_kernel, band=band),
        grid=(nb, n_qblocks // n_t, n_q),
        in_specs=[pl.BlockSpec((1, tq, qw), lambda b, j, i: (b, i, j)),
                  pl.BlockSpec((1, CTX_LEN, kvw), lambda b, j, i: (b, 0, kcol(j))),
                  pl.BlockSpec((1, CTX_LEN, kvw), lambda b, j, i: (b, 0, 2 // n_kv + kcol(j))),
                  pl.BlockSpec((1, t, kvw), lambda b, j, i: (b, 0, 8 // n_kv + kcol(j))),
                  pl.BlockSpec((1, t, kvw), lambda b, j, i: (b, 0, 10 // n_kv + kcol(j))),
                  pl.BlockSpec((1, 2 * tq, band), lambda b, j, i: (case(i), 0, 0)),
                  pl.BlockSpec(sinkv.shape, lambda b, j, i: (0, 0))],
        out_specs=pl.BlockSpec((1, tq, qw), lambda b, j, i: (b, i, j)),
        out_shape=jax.ShapeDtypeStruct((nb, t, n_qblocks * LANES), BF16),
        compiler_params=_params(3),
        name="attn_window",
    )(p_lat, p_ctx, p_ctx, p_lat, p_lat, bias, sinkv)


def _rope_tables(seq):
    rows = seq // GRID_W
    row = np.repeat(np.arange(rows, dtype=np.float64), GRID_W)
    col = np.tile(np.arange(GRID_W, dtype=np.float64), rows)
    inv = ROPE_THETA ** (-np.arange(0, ROPE_HALF, 2, dtype=np.float64) / ROPE_HALF)
    ang = np.concatenate([row[:, None] * inv, col[:, None] * inv], axis=-1)
    cos, sin = np.cos(ang), np.sin(ang)
    cos128 = np.tile(cos, (1, LANES // ROPE_HALF))
    sin128 = np.concatenate([-sin, -sin, sin, sin], axis=-1)
    return jnp.asarray(cos128, F32), jnp.asarray(sin128, F32)


_INTERLEAVE = np.concatenate([np.arange(0, 32), np.arange(64, 96), np.arange(32, 64), np.arange(96, 128)])


def _interleave_blocks(w, qk_blocks):
    idx = np.arange(w.shape[-1])
    for b in qk_blocks:
        idx[LANES * b:LANES * (b + 1)] = LANES * b + _INTERLEAVE
    return w[..., idx]


def _window_bias(seq, tq, band):
    n_q = seq // tq
    out = np.zeros((3, 2 * tq, band), np.float32)
    for case, i in enumerate((0, 1, n_q - 1)):
        start = min(max(i * tq - WINDOW, 0), seq - band)
        qpos = i * tq + np.arange(tq)[:, None]
        kpos = start + np.arange(band)[None, :]
        m = np.where(np.abs(kpos - qpos) <= WINDOW, 0.0, -np.inf).astype(np.float32)
        out[case] = np.concatenate([m, m], axis=0)
    return jnp.asarray(out)


def _pair_order(n_heads, n_kv):
    g = n_heads // n_kv
    order = []
    for kv in range(0, n_kv, 2):
        for t in range(g):
            order += [kv * g + t, (kv + 1) * g + t]
    return order


def _permute_heads(w, order, axis):
    shape = w.shape
    n = len(order)
    if axis == 1:
        return w.reshape(shape[0], n, HEAD_DIM)[:, np.asarray(order), :].reshape(shape)
    return w.reshape(n, HEAD_DIM, shape[1])[np.asarray(order)].reshape(shape)


def _tile_lanes(v):
    return jnp.tile(v, LANES // v.shape[-1])


def _pad_rows(rows, n=8):
    m = jnp.stack([_tile_lanes(r) if r.shape[-1] != LANES else r for r in rows]).astype(F32)
    return jnp.concatenate([m, jnp.zeros((n - m.shape[0], LANES), F32)], axis=0)


def kernel(x, c, ctx, c_ctx, w_mod, b_mod, g_pre, g_post, w_ffn_gate, w_ffn_up, w_ffn_down,
           w_in_ab, w_out_ab, q_gain_a, k_gain_a, lam_q1, lam_k1, lam_q2, lam_k2, sub_gain_b,
           w_in_c, w_out_c, sink_c):
    nb, seq, d = x.shape
    depth = w_mod.shape[0]
    assert depth == 2, "an odd layer that is not last would also need the windowed mixer's context output"
    n_ctx = ctx.shape[1]
    ctx_row = nb

    cc = jnp.concatenate([c, c_ctx[None, :], jnp.zeros((MOD_ROWS - nb - 1, d), F32)], axis=0)
    mod_all = _modulation(cc, w_mod, b_mod).reshape(depth, MOD_ROWS, 3 * N_SUB, d)

    cos_l, sin_l = _rope_tables(seq)
    scale = HEAD_DIM ** -0.5 * LOG2E

    lat_row = lambda b: b
    ctx_rowf = lambda b: ctx_row
    tm_proj = 512
    tm_ffn = 1024
    cos_c = jnp.ones((tm_proj, LANES), F32)
    sin_c = jnp.zeros((tm_proj, LANES), F32)
    x_lat = x
    x_ctx = ctx.reshape(1, nb * n_ctx, d)
    wg, wu, wd = w_ffn_gate.astype(BF16), w_ffn_up.astype(BF16), w_ffn_down.astype(BF16)

    for i in range(depth):
        last = i == depth - 1
        mod = mod_all[i]
        gpre, gpost = g_pre[i], g_post[i]

        if i % 2 == 0:
            e = i // 2
            lam_init = 0.8 - 0.6 * math.exp(-0.3 * i)
            order_a = _pair_order(A_HEADS, A_KV_HEADS)
            n_qa = A_HEADS * HEAD_DIM
            w_in = w_in_ab[e]
            w_in_lat = jnp.concatenate([_permute_heads(w_in[:, :n_qa], order_a, 1), w_in[:, n_qa:]], axis=1)
            w_in_lat = _interleave_blocks(w_in_lat, [0, 1, 2, 3, 4, 6, 7, 8, 9, 10, 11, 12, 13]).astype(BF16)
            w_in_ctx = w_in_lat
            w_out = w_out_ab[e]
            w_out = jnp.concatenate([_permute_heads(w_out[:n_qa], order_a, 0), w_out[n_qa:]],
                                    axis=0).astype(BF16)
            gains = _interleave_blocks(_pad_rows([q_gain_a[e], k_gain_a[e]]), [0])
            lamv = _pad_rows([jnp.pad(v, (0, LANES - HEAD_DIM))
                              for v in (lam_q1[e], lam_k1[e], lam_q2[e], lam_k2[e])])
            subg = _pad_rows([sub_gain_b[e]])
            blocks = ([(0, True, scale)] * 4 + [(1, True, 1.0), (None, False, 1.0)]
                      + [(None, True, scale)] * 4 + [(None, True, 1.0)] * 4 + [(None, False, 1.0)] * 4)
            blocks_ctx = [(g, False, s) for g, _, s in blocks]
        else:
            o = i // 2
            order_c = _pair_order(C_HEADS, C_KV_HEADS)
            n_qc = C_HEADS * HEAD_DIM
            w_in = w_in_c[o]
            w_in_lat = jnp.concatenate([_permute_heads(w_in[:, :n_qc], order_c, 1), w_in[:, n_qc:]], axis=1)
            w_in_lat = _interleave_blocks(w_in_lat, range(10)).astype(BF16)
            w_in_ctx = w_in_lat[:, n_qc:]
            w_out = _permute_heads(w_out_c[o], order_c, 0).astype(BF16)
            sink = sink_c[o][np.asarray(order_c)].astype(F32)
            sinkv = jnp.repeat(sink, LANES).reshape(C_HEADS, LANES)
            gains = jnp.zeros((8, LANES), F32)
            blocks = [(None, True, scale)] * 8 + [(None, True, 1.0)] * 2 + [(None, False, 1.0)] * 2
            blocks_ctx = [(None, False, 1.0)] * 4

        x_lat = _ffn(x_lat, mod, lat_row, gpre, gpost, wg, wu, wd, (i, 0), 0, tm_ffn)
        x_ctx = _ffn(x_ctx, mod, ctx_rowf, gpre, gpost, wg, wu, wd, (i, 0), 0, tm_ffn)
        p_lat = _proj(x_lat, mod, lat_row, gpre, w_in_lat, cos_l, sin_l, gains, blocks, tm_proj, True)
        p_ctx = _proj(x_ctx, mod, ctx_rowf, gpre, w_in_ctx, cos_c, sin_c, gains, blocks_ctx, tm_proj,
                      False).reshape(nb, n_ctx, -1)

        if i % 2 == 0:
            qa = lambda j: j
            qb, kb, vb = (lambda j: 3 + j), (lambda j: 5 + j), (lambda j: 7 + j)
            a_lat = _attn_ab("gqa", p_lat, [p_ctx, p_lat], qa, 4, 5, 256, n_t=4)
            b_lat = _attn_ab("diff", p_lat, [p_ctx, p_lat], qb, kb, vb, 512, (lamv, subg), lam_init, n_t=2)
            mix_lat = [a_lat, b_lat]
            if not last:
                a_ctx = _attn_ab("gqa", p_ctx, [p_ctx], qa, 4, 5, n_ctx, n_t=4)
                b_ctx = _attn_ab("diff", p_ctx, [p_ctx], qb, kb, vb, n_ctx, (lamv, subg), lam_init, n_t=2)
                mix_ctx = [a_ctx.reshape(1, nb * n_ctx, -1), b_ctx.reshape(1, nb * n_ctx, -1)]
        else:
            tq = 256
            band = tq + 2 * WINDOW
            c_lat = _attn_window(p_lat, p_ctx, _window_bias(seq, tq, band), sinkv, tq, band, 8)
            mix_lat = [c_lat]

        x_lat = _ffn(x_lat, mod, lat_row, gpre, gpost, wg, wu, wd, (i, 1), 2, tm_ffn, mix_lat, w_out)
        if not last:
            x_ctx = _ffn(x_ctx, mod, ctx_rowf, gpre, gpost, wg, wu, wd, (i, 1), 2, tm_ffn, mix_ctx, w_out)
    return x_lat
```
